```python
import math, functools
import jax, jax.numpy as jnp
from jax import lax
import numpy as np

D_MODEL = 1024
BATCH = 4
SEQ = 8192
DEPTH = 1
DEC_BATCH = 128
DEC_SEQ = 4
PAST_LEN = 8192
PAGE_SIZE = 128

SB_HEADS = 8
SB_HEAD_DIM = 64
SB_WIDTH = SB_HEADS * SB_HEAD_DIM
SB_BLOCK_Q = 128
SSD_HEADS = 8
SSD_HEAD_DIM = 64
SSD_WIDTH = SSD_HEADS * SSD_HEAD_DIM
SSD_GROUPS = 2
SSD_STATE = 128
SSD_CHUNK = 128
CONV_WIDTH = 4
CONV_CH = SSD_WIDTH + 2 * SSD_GROUPS * SSD_STATE
D_MIX = SB_WIDTH + SSD_WIDTH
IN_COLS = 3 * SB_WIDTH + SSD_WIDTH + CONV_CH + SSD_HEADS
N_EXPERTS = 32
TOP_K = 4
D_FF = D_MODEL
MOE_BLOCK = 128
SWIGLU_ALPHA = 1.702
SWIGLU_LIMIT = 7.0
N_MOD = 6
NORM_EPS = 1e-6

kernel_name = 'hymba_stickbreak_ssd_moe_step'


def _rmsnorm(x, g):
    xf = x.astype(jnp.float32)
    y = xf * lax.rsqrt(jnp.mean(xf * xf, axis=-1, keepdims=True) + NORM_EPS)
    return (y * g.astype(jnp.float32)).astype(x.dtype)


def _group_rmsnorm(x, g, groups):
    shp = x.shape
    xf = x.astype(jnp.float32).reshape(shp[:-1] + (groups, shp[-1] // groups))
    y = xf * lax.rsqrt(jnp.mean(xf * xf, axis=-1, keepdims=True) + NORM_EPS)
    return y.reshape(shp) * g.astype(jnp.float32)


def _modulate(x, g, shift, scale):
    h = _rmsnorm(x, g)
    return (h * (1 + scale[:, None, :]) + shift[:, None, :]).astype(x.dtype)


def _ada(c, w, b, n):
    m = jnp.einsum('bd,dm->bm', jax.nn.silu(c), w) + b
    return jnp.split(m, n, axis=-1)


def _sb_logits(q, k, bias):
    z = jnp.einsum('bqhd,bkhd->bhqk', q, k, preferred_element_type=jnp.float32) * (1.0 / math.sqrt(q.shape[-1]))
    return z + bias.astype(jnp.float32)[None, :, None, None]


def _sb_weights(z, mask, carry):
    last = z.ndim - 1
    log_rest = jnp.where(mask, jax.nn.log_sigmoid(-z), 0.0)
    later = lax.cumsum(log_rest, axis=last, reverse=True) - log_rest + carry[..., None]
    w = jnp.where(mask, jnp.exp(jax.nn.log_sigmoid(z) + later), 0.0)
    return w, carry + jnp.sum(log_rest, axis=-1)


def _sb_prompt(q, k, v, bias):
    b, t, h, hd = q.shape
    nqb = t // SB_BLOCK_Q
    kpos = jnp.arange(t, dtype=jnp.int32)
    qb = q.reshape(b, nqb, SB_BLOCK_Q, h, hd).swapaxes(0, 1)
    qpos = kpos.reshape(nqb, SB_BLOCK_Q)
    vf = v.astype(jnp.float32)

    def block(args):
        qblk, qp = args
        z = _sb_logits(qblk, k, bias)
        mask = kpos[None, :] < qp[:, None]
        w, _ = _sb_weights(z, mask, jnp.zeros(z.shape[:-1], jnp.float32))
        return jnp.einsum('bhqk,bkhd->bqhd', w, vf)

    o = lax.map(block, (qb, qpos))
    return o.swapaxes(0, 1).reshape(b, t, h, hd)


def _sb_sample(q, k, v, bias, cache_k, cache_v, page_table, layer):
    b, s, h, hd = q.shape
    pos = jnp.arange(s, dtype=jnp.int32)
    z = _sb_logits(q, k, bias)
    w, carry = _sb_weights(z, pos[None, :] < pos[:, None], jnp.zeros(z.shape[:-1], jnp.float32))
    o = jnp.einsum('bhqk,bkhd->bqhd', w, v.astype(jnp.float32))

    def page_step(st, ids):
        o_acc, c = st
        kp = cache_k[layer, ids]
        vp = cache_v[layer, ids]
        zp = _sb_logits(q, kp, bias)
        wp, c = _sb_weights(zp, True, c)
        return (o_acc + jnp.einsum('bhqk,bkhd->bqhd', wp, vp.astype(jnp.float32)), c), None

    (o, _), _ = lax.scan(page_step, (o, carry), page_table.T[::-1])
    return o


def _causal_conv(xbc, prev, w, bias):
    full = jnp.concatenate([prev.astype(xbc.dtype), xbc], axis=1)
    y = lax.conv_general_dilated(full, w.astype(xbc.dtype)[:, None, :], window_strides=(1,), padding='VALID',
                                 dimension_numbers=('NWC', 'WIO', 'NWC'), feature_group_count=CONV_CH)
    return jax.nn.silu(y + bias), full[:, full.shape[1] - (CONV_WIDTH - 1):]


def _segsum(x):
    t = x.shape[-1]
    xx = jnp.broadcast_to(x[..., :, None], x.shape + (t,))
    xx = jnp.where(jnp.tril(jnp.ones((t, t), bool), -1), xx, 0.0)
    s = jnp.cumsum(xx, axis=-2)
    return jnp.where(jnp.tril(jnp.ones((t, t), bool)), s, -jnp.inf)


def _ssd(x, dt, a, bm, cm, h0):
    b, L, H, P = x.shape
    G, N = bm.shape[2], bm.shape[3]
    R = H // G
    Q = SSD_CHUNK if L % SSD_CHUNK == 0 else L
    nc = L // Q
    x = x.astype(jnp.float32).reshape(b, nc, Q, G, R, P)
    dt = dt.reshape(b, nc, Q, G, R)
    bm = bm.astype(jnp.float32).reshape(b, nc, Q, G, N)
    cm = cm.astype(jnp.float32).reshape(b, nc, Q, G, N)
    xdt = x * dt[..., None]
    da = (dt * a.astype(jnp.float32).reshape(G, R)).transpose(0, 3, 4, 1, 2)
    a_cum = jnp.cumsum(da, axis=-1)
    decay_in = jnp.exp(_segsum(da))
    cb = jnp.einsum('bclgn,bcsgn->bgcls', cm, bm)
    y_diag = jnp.einsum('bgcls,bgrcls,bcsgrp->bclgrp', cb, decay_in, xdt)
    decay_states = jnp.exp(a_cum[..., -1:] - a_cum)
    states = jnp.einsum('bcsgn,bgrcs,bcsgrp->bcgrpn', bm, decay_states, xdt)
    h0 = h0.astype(jnp.float32).reshape(b, G, R, P, N)
    states = jnp.concatenate([h0[:, None], states], axis=1)
    chunk_tot = jnp.pad(a_cum[..., -1], ((0, 0), (0, 0), (0, 0), (1, 0)))
    decay_chunk = jnp.exp(_segsum(chunk_tot))
    states = jnp.einsum('bgrzc,bcgrpn->bzgrpn', decay_chunk, states)
    y_off = jnp.einsum('bclgn,bcgrpn,bgrcl->bclgrp', cm, states[:, :-1], jnp.exp(a_cum))
    y = (y_diag + y_off).reshape(b, L, H, P)
    return y, states[:, -1].reshape(b, H, P, N)


def _hybrid_mixer(h, w_in, conv_w, conv_b, dt_bias, a_log, d_skip, sb_bias, g_sb, g_ssd, w_out, conv_prev, h0, attend):
    b, L, _ = h.shape
    proj = jnp.einsum('bld,dc->blc', h, w_in)
    i3 = 3 * SB_WIDTH
    q, k, v, z, xbc, dt_raw = jnp.split(proj, [SB_WIDTH, 2 * SB_WIDTH, i3, i3 + SSD_WIDTH, i3 + SSD_WIDTH + CONV_CH], axis=-1)
    q = q.reshape(b, L, SB_HEADS, SB_HEAD_DIM)
    k = k.reshape(b, L, SB_HEADS, SB_HEAD_DIM)
    v = v.reshape(b, L, SB_HEADS, SB_HEAD_DIM)
    o_sb = attend(q, k, v, sb_bias).reshape(b, L, SB_WIDTH)
    xbc, new_conv = _causal_conv(xbc, conv_prev, conv_w, conv_b)
    xs, bm, cm = jnp.split(xbc, [SSD_WIDTH, SSD_WIDTH + SSD_GROUPS * SSD_STATE], axis=-1)
    xs = xs.reshape(b, L, SSD_HEADS, SSD_HEAD_DIM)
    dt = jax.nn.softplus(dt_raw.astype(jnp.float32) + dt_bias.astype(jnp.float32))
    a = -jnp.exp(a_log.astype(jnp.float32))
    y, h_fin = _ssd(xs, dt, a, bm.reshape(b, L, SSD_GROUPS, SSD_STATE), cm.reshape(b, L, SSD_GROUPS, SSD_STATE), h0)
    y = y + d_skip.astype(jnp.float32)[:, None] * xs.astype(jnp.float32)
    y = y.reshape(b, L, SSD_WIDTH) * jax.nn.silu(z.astype(jnp.float32))
    y = _group_rmsnorm(y, g_ssd, SSD_GROUPS)
    o = jnp.concatenate([_rmsnorm(o_sb, g_sb), y], axis=-1).astype(h.dtype)
    out = jnp.einsum('blm,md->bld', o, w_out)
    return out, k, v, h_fin.astype(h0.dtype), new_conv


def _clamped_swiglu(hdn):
    x_glu = jnp.minimum(hdn[..., ::2], SWIGLU_LIMIT)
    x_lin = jnp.clip(hdn[..., 1::2], -SWIGLU_LIMIT, SWIGLU_LIMIT)
    return x_glu * jax.nn.sigmoid(SWIGLU_ALPHA * x_glu) * (x_lin + 1.0)


def _moe(x, w_router, b_router, w1, b1, w2, b2):
    n_tok, d = x.shape
    n_exp = w_router.shape[-1]
    logits = jnp.einsum('td,de->te', x, w_router, preferred_element_type=jnp.float32) + b_router.astype(jnp.float32)
    top_val, top_idx = lax.top_k(logits, TOP_K)
    gates = jax.nn.softmax(top_val, axis=-1)
    n_asg = n_tok * TOP_K
    flat_e = top_idx.reshape(-1)
    flat_tok = jnp.arange(n_asg, dtype=jnp.int32) // TOP_K
    order = jnp.argsort(flat_e)
    sorted_e = flat_e[order]
    counts = jnp.bincount(flat_e, length=n_exp)
    padded = (counts + MOE_BLOCK - 1) // MOE_BLOCK * MOE_BLOCK
    pad_end = jnp.cumsum(padded)
    pad_start = pad_end - padded
    start = jnp.cumsum(counts) - counts
    dest = pad_start[sorted_e] + jnp.arange(n_asg, dtype=jnp.int32) - start[sorted_e]
    n_blocks = -(-(n_asg + n_exp * (MOE_BLOCK - 1)) // MOE_BLOCK)
    n_slots = n_blocks * MOE_BLOCK
    slot_tok = jnp.full((n_slots,), n_tok, jnp.int32).at[dest].set(flat_tok[order])
    slot_gate = jnp.zeros((n_slots,), jnp.float32).at[dest].set(gates.reshape(-1)[order])
    block_exp = jnp.minimum(jnp.searchsorted(pad_end, jnp.arange(n_blocks, dtype=jnp.int32) * MOE_BLOCK, side='right'), n_exp - 1)
    x_pad = jnp.concatenate([x, jnp.zeros((1, d), x.dtype)], axis=0)

    def expert_block(args):
        tok, gate, e = args
        hdn = x_pad[tok] @ w1[e] + b1[e]
        out = _clamped_swiglu(hdn) @ w2[e] + b2[e]
        return out.astype(jnp.float32) * gate[:, None]

    outs = lax.map(expert_block, (slot_tok.reshape(n_blocks, MOE_BLOCK), slot_gate.reshape(n_blocks, MOE_BLOCK), block_exp))
    y = jnp.zeros((n_tok + 1, d), jnp.float32).at[slot_tok].add(outs.reshape(n_slots, d))
    return y[:n_tok].astype(x.dtype)


def setup_inputs(seed: int = 0) -> dict:
    key = jax.random.key(seed)
    ks = jax.random.split(key, 32)
    f32 = jnp.float32
    n_pages = PAST_LEN // PAGE_SIZE
    n_pool = (DEC_BATCH * n_pages * 5) // 4

    def nrm(k, shape, s):
        return jax.random.normal(k, shape, f32) * s

    dt0 = jnp.exp(jax.random.uniform(ks[15], (DEPTH, SSD_HEADS), f32, math.log(1e-3), math.log(1e-1)))
    return {
        'x_prompt': nrm(ks[0], (BATCH, SEQ, D_MODEL), 1.0),
        'x_sample': nrm(ks[1], (DEC_BATCH, DEC_SEQ, D_MODEL), 1.0),
        'c_prompt': nrm(ks[2], (BATCH, D_MODEL), 1.0),
        'c_sample': nrm(ks[3], (DEC_BATCH, D_MODEL), 1.0),
        'cache_k': nrm(ks[4], (DEPTH, n_pool, PAGE_SIZE, SB_HEADS, SB_HEAD_DIM), 1.0),
        'cache_v': nrm(ks[5], (DEPTH, n_pool, PAGE_SIZE, SB_HEADS, SB_HEAD_DIM), 1.0),
        'state_ssm': nrm(ks[6], (DEPTH, DEC_BATCH, SSD_HEADS, SSD_HEAD_DIM, SSD_STATE), 0.1),
        'state_conv': nrm(ks[7], (DEPTH, DEC_BATCH, CONV_WIDTH - 1, CONV_CH), 1.0),
        'page_table': jax.random.permutation(ks[8], n_pool)[:DEC_BATCH * n_pages].reshape(DEC_BATCH, n_pages).astype(jnp.int32),
        'w_ada': nrm(ks[9], (DEPTH, D_MODEL, N_MOD * D_MODEL), 0.5 * D_MODEL ** -0.5),
        'b_ada': nrm(ks[10], (DEPTH, N_MOD * D_MODEL), 0.02),
        'g_norm1': 1.0 + nrm(ks[11], (DEPTH, D_MODEL), 0.02),
        'w_in': nrm(ks[12], (DEPTH, D_MODEL, IN_COLS), D_MODEL ** -0.5),
        'conv_w': nrm(ks[13], (DEPTH, CONV_WIDTH, CONV_CH), CONV_WIDTH ** -0.5),
        'conv_b': nrm(ks[14], (DEPTH, CONV_CH), 0.02),
        'dt_bias': dt0 + jnp.log(-jnp.expm1(-dt0)),
        'a_log': jnp.log(jax.random.uniform(ks[16], (DEPTH, SSD_HEADS), f32, 1.0, 16.0)),
        'd_skip': 1.0 + nrm(ks[17], (DEPTH, SSD_HEADS), 0.1),
        'sb_bias': -jax.random.uniform(ks[31], (DEPTH, SB_HEADS), f32, 7.0, 9.0),
        'g_sb': 1.0 + nrm(ks[18], (DEPTH, SB_WIDTH), 0.02),
        'g_ssd': 1.0 + nrm(ks[19], (DEPTH, SSD_WIDTH), 0.02),
        'w_out': nrm(ks[20], (DEPTH, D_MIX, D_MODEL), D_MIX ** -0.5),
        'g_norm2': 1.0 + nrm(ks[21], (DEPTH, D_MODEL), 0.02),
        'w_router': nrm(ks[22], (DEPTH, D_MODEL, N_EXPERTS), D_MODEL ** -0.5),
        'b_router': nrm(ks[23], (DEPTH, N_EXPERTS), 0.01),
        'w1': nrm(ks[24], (DEPTH, N_EXPERTS, D_MODEL, 2 * D_FF), D_MODEL ** -0.5),
        'b1': nrm(ks[25], (DEPTH, N_EXPERTS, 2 * D_FF), 0.02),
        'w2': nrm(ks[26], (DEPTH, N_EXPERTS, D_FF, D_MODEL), D_FF ** -0.5),
        'b2': nrm(ks[27], (DEPTH, N_EXPERTS, D_MODEL), 0.02),
        'w_ada_final': nrm(ks[28], (D_MODEL, 2 * D_MODEL), 0.5 * D_MODEL ** -0.5),
        'b_ada_final': nrm(ks[29], (2 * D_MODEL,), 0.02),
        'g_final': 1.0 + nrm(ks[30], (D_MODEL,), 0.02),
    }


def reference(x_prompt, x_sample, c_prompt, c_sample, cache_k, cache_v, state_ssm, state_conv, page_table,
              w_ada, b_ada, g_norm1, w_in, conv_w, conv_b, dt_bias, a_log, d_skip, sb_bias, g_sb, g_ssd, w_out,
              g_norm2, w_router, b_router, w1, b1, w2, b2, w_ada_final, b_ada_final, g_final):
    xp, xs = x_prompt, x_sample
    bp, tp = xp.shape[0], xp.shape[1]
    bs, ts = xs.shape[0], xs.shape[1]
    kps, vps, sps, cps, kss, vss, sss, css = [], [], [], [], [], [], [], []
    for l in range(DEPTH):
        mp = _ada(c_prompt, w_ada[l], b_ada[l], N_MOD)
        ms = _ada(c_sample, w_ada[l], b_ada[l], N_MOD)
        mix_w = (w_in[l], conv_w[l], conv_b[l], dt_bias[l], a_log[l], d_skip[l], sb_bias[l], g_sb[l], g_ssd[l], w_out[l])
        hp = _modulate(xp, g_norm1[l], mp[0], mp[1])
        conv0 = jnp.zeros((bp, CONV_WIDTH - 1, CONV_CH), xp.dtype)
        ssm0 = jnp.zeros((bp, SSD_HEADS, SSD_HEAD_DIM, SSD_STATE), state_ssm.dtype)
        out_p, k_p, v_p, s_p, c_p = _hybrid_mixer(hp, *mix_w, conv0, ssm0, _sb_prompt)
        hs = _modulate(xs, g_norm1[l], ms[0], ms[1])
        attend_s = functools.partial(_sb_sample, cache_k=cache_k, cache_v=cache_v, page_table=page_table, layer=l)
        out_s, k_s, v_s, s_s, c_s = _hybrid_mixer(hs, *mix_w, state_conv[l], state_ssm[l], attend_s)
        xp = xp + mp[2][:, None, :] * out_p
        xs = xs + ms[2][:, None, :] * out_s
        kps.append(k_p); vps.append(v_p); sps.append(s_p); cps.append(c_p)
        kss.append(k_s); vss.append(v_s); sss.append(s_s); css.append(c_s)
        h2p = _modulate(xp, g_norm2[l], mp[3], mp[4]).reshape(bp * tp, D_MODEL)
        h2s = _modulate(xs, g_norm2[l], ms[3], ms[4]).reshape(bs * ts, D_MODEL)
        ff = _moe(jnp.concatenate([h2p, h2s], axis=0), w_router[l], b_router[l], w1[l], b1[l], w2[l], b2[l])
        xp = xp + mp[5][:, None, :] * ff[:bp * tp].reshape(bp, tp, D_MODEL)
        xs = xs + ms[5][:, None, :] * ff[bp * tp:].reshape(bs, ts, D_MODEL)
    fp = _ada(c_prompt, w_ada_final, b_ada_final, 2)
    fs = _ada(c_sample, w_ada_final, b_ada_final, 2)
    y_prompt = _modulate(xp, g_final, fp[0], fp[1])
    y_sample = _modulate(xs, g_final, fs[0], fs[1])
    return (y_prompt, y_sample, jnp.stack(kps), jnp.stack(vps), jnp.stack(sps), jnp.stack(cps),
            jnp.stack(kss), jnp.stack(vss), jnp.stack(sss), jnp.stack(css))
```

```python
import functools
import math

import jax
import jax.numpy as jnp
from jax import lax
from jax.experimental import pallas as pl
from jax.experimental.pallas import tpu as pltpu

F32 = jnp.float32
BF16 = jnp.bfloat16
I32 = jnp.int32

NORM_EPS = 1e-6
N_MOD = 6
TOP_K = 4
SWIGLU_ALPHA = 1.702
SWIGLU_LIMIT = 7.0
LOG2E = 1.4426950408889634

LANES = 128
SUBLANES = 8
HEAD_PAIR = 2
SSD_CHUNK = 128
CONV_HALO = SUBLANES
ATTN_TQ = 256
ATTN_TK = 256
PROJ_TM = 512
MOE_BM = 256
COMB_TM = 128
DEC_PAGES_PER_STEP = 8
VMEM_LIMIT = 56 * 1024 * 1024


def _cparams(sem):
    return pltpu.CompilerParams(dimension_semantics=sem, vmem_limit_bytes=VMEM_LIMIT)


def _silu(x):
    return x * jax.nn.sigmoid(x)


def _softplus(x):
    return jnp.maximum(x, 0.0) + jnp.log1p(jnp.exp(-jnp.abs(x)))


def _split3(x):
    hi = x.astype(BF16)
    r1 = x - hi.astype(F32)
    mid = r1.astype(BF16)
    lo = (r1 - mid.astype(F32)).astype(BF16)
    return hi, mid, lo


def _dot(a, b):
    return jnp.dot(a, b, preferred_element_type=F32)


def _dot_nt(a, b):
    return lax.dot_general(a, b, (((1,), (1,)), ((), ())), preferred_element_type=F32)


def _dot_tn(a, b):
    return lax.dot_general(a, b, (((0,), (0,)), ((), ())), preferred_element_type=F32)


def _dot3_left(x, m_bf16):
    hi, mid, lo = _split3(x)
    return _dot(hi, m_bf16) + _dot(mid, m_bf16) + _dot(lo, m_bf16)


def _dot3_right(m_bf16, x):
    hi, mid, lo = _split3(x)
    return _dot(m_bf16, hi) + _dot(m_bf16, mid) + _dot(m_bf16, lo)


def _ada_kernel(c_ref, w_ref, b_ref, o_ref):
    s = _silu(c_ref[...])
    o_ref[...] = _dot(s.astype(BF16), w_ref[...].astype(BF16)) + b_ref[...]


def _ada(c, w, b):
    r, d = c.shape
    m = w.shape[1]
    tn = 1024
    return pl.pallas_call(
        _ada_kernel,
        grid=(m // tn,),
        in_specs=[pl.BlockSpec((r, d), lambda j: (0, 0)),
                  pl.BlockSpec((d, tn), lambda j: (0, j)),
                  pl.BlockSpec((1, tn), lambda j: (0, j))],
        out_specs=pl.BlockSpec((r, tn), lambda j: (0, j)),
        out_shape=jax.ShapeDtypeStruct((r, m), F32),
        compiler_params=_cparams(("arbitrary",)),
        name="ada",
    )(c, w, b.reshape(1, m))


def _inproj_kernel(x_ref, sh_ref, sc_ref, g_ref, w_ref, wdt_ref, wdtt_ref, dtb_ref, dtbt_ref,
                   q_ref, k_ref, v_ref, kb_ref, vb_ref, z_ref, xbc_ref, dt_ref, dtt_ref,
                   *, sbw, ssdw, convc, qscale):
    x = x_ref[...]
    h = x * lax.rsqrt(jnp.mean(x * x, axis=-1, keepdims=True) + NORM_EPS) * g_ref[...]
    h = h * (1.0 + sc_ref[...]) + sh_ref[...]
    hb = h.astype(BF16)
    q = _dot(hb, w_ref[:, 0:sbw])
    q_ref[...] = (q * qscale).astype(BF16)
    k = _dot(hb, w_ref[:, sbw:2 * sbw])
    k_ref[...] = k
    kb_ref[...] = k.astype(BF16)
    v = _dot(hb, w_ref[:, 2 * sbw:3 * sbw])
    v_ref[...] = v
    vb_ref[...] = v.astype(BF16)
    o = 3 * sbw
    z_ref[...] = _dot(hb, w_ref[:, o:o + ssdw])
    xbc_ref[...] = _dot(hb, w_ref[:, o + ssdw:o + ssdw + convc])
    dt_ref[...] = _softplus(_dot(hb, wdt_ref[...]) + dtb_ref[...])
    dtt_ref[...] = _softplus(_dot_nt(wdtt_ref[...], hb) + dtbt_ref[...])


def _inproj(x, shift, scale, g, w_main, w_dt, w_dtt, dtb, dtbt, *, per_row, tokens_per_batch,
            sbw, ssdw, convc, qscale):
    t, d = x.shape
    tm = min(PROJ_TM, t)
    grid = (t // tm,)
    if per_row:
        mod_spec = pl.BlockSpec((tm, d), lambda i: (i, 0))
    else:
        tiles_per_batch = tokens_per_batch // tm
        mod_spec = pl.BlockSpec((None, 1, d), lambda i: (i // tiles_per_batch, 0, 0))
    full = lambda a: pl.BlockSpec(a.shape, lambda i: (0,) * a.ndim)
    row = lambda n: pl.BlockSpec((tm, n), lambda i: (i, 0))
    out_shape = [
        jax.ShapeDtypeStruct((t, sbw), BF16),
        jax.ShapeDtypeStruct((t, sbw), F32),
        jax.ShapeDtypeStruct((t, sbw), F32),
        jax.ShapeDtypeStruct((t, sbw), BF16),
        jax.ShapeDtypeStruct((t, sbw), BF16),
        jax.ShapeDtypeStruct((t, ssdw), F32),
        jax.ShapeDtypeStruct((t, convc), F32),
        jax.ShapeDtypeStruct((t, LANES), F32),
        jax.ShapeDtypeStruct((LANES, t), F32),
    ]
    out_specs = [row(sbw), row(sbw), row(sbw), row(sbw), row(sbw), row(ssdw), row(convc), row(LANES),
                 pl.BlockSpec((LANES, tm), lambda i: (0, i))]
    return pl.pallas_call(
        functools.partial(_inproj_kernel, sbw=sbw, ssdw=ssdw, convc=convc, qscale=qscale),
        grid=grid,
        in_specs=[row(d), mod_spec, mod_spec, full(g), full(w_main), full(w_dt), full(w_dtt), full(dtb),
                  full(dtbt)],
        out_specs=out_specs,
        out_shape=out_shape,
        compiler_params=_cparams(("arbitrary",)),
        name="inproj",
    )(x, shift, scale, g, w_main, w_dt, w_dtt, dtb, dtbt)


def _sb_tile(qm, kb, vb, tri, bias, mask):
    s = _dot_nt(qm, kb) + bias
    e = jnp.exp2(-jnp.abs(s))
    l2 = jnp.log2(1.0 + e)
    ls = jnp.minimum(s, 0.0) - l2
    lr = ls - s
    if mask is not None:
        lr = jnp.where(mask, lr, 0.0)
    later = _dot(lr.astype(BF16), tri)
    w = jnp.exp2(ls + later)
    if mask is not None:
        w = jnp.where(mask, w, 0.0)
    tot = jnp.sum(lr, axis=-1, keepdims=True)
    return _dot(w.astype(BF16), vb), tot


def _strict_upper_ones(n):
    r = lax.broadcasted_iota(I32, (n, n), 0)
    c = lax.broadcasted_iota(I32, (n, n), 1)
    return jnp.where(r > c, 1.0, 0.0).astype(BF16)


def _sb_prompt_kernel(bias_ref, q_ref, k_ref, v_ref, o_ref, *, tq, hd):
    hp = pl.program_id(1)
    qi = pl.program_id(2)
    lane = lax.broadcasted_iota(I32, (tq, LANES), 1)
    r = lax.broadcasted_iota(I32, (tq, tq), 0)
    c = lax.broadcasted_iota(I32, (tq, tq), 1)
    diag_mask = c < r
    tri = _strict_upper_ones(tq)
    q = q_ref[...]
    qms = [jnp.where((lane >= hd * j) & (lane < hd * (j + 1)), q, jnp.zeros_like(q)) for j in range(HEAD_PAIR)]
    biases = [bias_ref[hp * HEAD_PAIR + j] for j in range(HEAD_PAIR)]

    start0 = pl.multiple_of(qi * tq, tq)
    kd = k_ref[pl.ds(start0, tq), :]
    vd = v_ref[pl.ds(start0, tq), :]
    carry = []
    for j in range(HEAD_PAIR):
        pv, tot = _sb_tile(qms[j], kd, vd, tri, biases[j], diag_mask)
        carry += [pv, tot]

    def body(i, carry):
        start = pl.multiple_of((qi - 1 - i) * tq, tq)
        kb = k_ref[pl.ds(start, tq), :]
        vb = v_ref[pl.ds(start, tq), :]
        new = []
        for j in range(HEAD_PAIR):
            acc, cs = carry[2 * j], carry[2 * j + 1]
            pv, tot = _sb_tile(qms[j], kb, vb, tri, biases[j], None)
            new += [acc + jnp.exp2(cs) * pv, cs + tot]
        return tuple(new)

    carry = lax.fori_loop(0, qi, body, tuple(carry))
    o_ref[...] = jnp.where(lane < hd, carry[0], carry[2])


def _sb_prompt(q, kb, vb, bias2, *, hd):
    b, t, w = q.shape
    tq = min(ATTN_TQ, t)
    n_pairs = w // LANES
    return pl.pallas_call(
        functools.partial(_sb_prompt_kernel, tq=tq, hd=hd),
        grid=(b, n_pairs, t // tq),
        in_specs=[pl.BlockSpec(memory_space=pltpu.SMEM),
                  pl.BlockSpec((None, tq, LANES), lambda bi, hp, qi: (bi, qi, hp)),
                  pl.BlockSpec((None, t, LANES), lambda bi, hp, qi: (bi, 0, hp)),
                  pl.BlockSpec((None, t, LANES), lambda bi, hp, qi: (bi, 0, hp))],
        out_specs=pl.BlockSpec((None, tq, LANES), lambda bi, hp, qi: (bi, qi, hp)),
        out_shape=jax.ShapeDtypeStruct((b, t, w), F32),
        compiler_params=_cparams(("arbitrary", "arbitrary", "arbitrary")),
        name="sb_prompt",
    )(bias2, q, kb, vb)


def _sb_decode_kernel(pt_ref, qbd_ref, bias_ref, kn_ref, vn_ref, *refs, n_pages_step, n_heads, hd, page):
    k_refs = refs[:n_pages_step]
    v_refs = refs[n_pages_step:2 * n_pages_step]
    o_ref = refs[2 * n_pages_step]
    acc_ref, c_ref = refs[2 * n_pages_step + 1:]
    g = pl.program_id(1)
    n_rows = qbd_ref.shape[0]
    qbd = qbd_ref[...]
    bias = bias_ref[...]
    tri = _strict_upper_ones(page)

    @pl.when(g == 0)
    def _():
        rowq = lax.broadcasted_iota(I32, (n_rows, page), 0) // n_heads
        keyi = lax.broadcasted_iota(I32, (n_rows, page), 1)
        pv, tot = _sb_tile(qbd, kn_ref[...].astype(BF16), vn_ref[...].astype(BF16), tri, bias, keyi < rowq)
        acc_ref[...] = pv
        c_ref[...] = tot

    acc = acc_ref[...]
    cs = c_ref[...]
    for j in range(n_pages_step):
        pv, tot = _sb_tile(qbd, k_refs[j][...].astype(BF16), v_refs[j][...].astype(BF16), tri, bias, None)
        acc = acc + jnp.exp2(cs) * pv
        cs = cs + tot
    acc_ref[...] = acc
    c_ref[...] = cs

    @pl.when(g == pl.num_programs(1) - 1)
    def _():
        w = acc.shape[1]
        hr = lax.broadcasted_iota(I32, (n_rows, w), 0) % n_heads
        hl = lax.broadcasted_iota(I32, (n_rows, w), 1) // hd
        om = jnp.where(hr == hl, acc, 0.0)
        o_ref[...] = jnp.sum(om.reshape(n_rows // n_heads, n_heads, w), axis=1)


def _sb_decode(qbd, bias_col, k_new, v_new, cache_k, cache_v, page_table, *, n_heads, hd):
    s, n_rows, w = qbd.shape
    page = cache_k.shape[1]
    n_pages = page_table.shape[1]
    pps = min(DEC_PAGES_PER_STEP, n_pages)
    n_groups = n_pages // pps
    n_tok = n_rows // n_heads

    def page_spec(j):
        return pl.BlockSpec((None, page, w), lambda si, g, pt: (pt[si, n_pages - 1 - (g * pps + j)], 0, 0))

    seq3 = lambda r: pl.BlockSpec((None, r, w), lambda si, g, pt: (si, 0, 0))
    grid_spec = pltpu.PrefetchScalarGridSpec(
        num_scalar_prefetch=1,
        grid=(s, n_groups),
        in_specs=[seq3(n_rows), pl.BlockSpec((n_rows, 1), lambda si, g, pt: (0, 0)), seq3(page), seq3(page)]
        + [page_spec(j) for j in range(pps)] + [page_spec(j) for j in range(pps)],
        out_specs=seq3(n_tok),
        scratch_shapes=[pltpu.VMEM((n_rows, w), F32), pltpu.VMEM((n_rows, 1), F32)],
    )
    return pl.pallas_call(
        functools.partial(_sb_decode_kernel, n_pages_step=pps, n_heads=n_heads, hd=hd, page=page),
        grid_spec=grid_spec,
        out_shape=jax.ShapeDtypeStruct((s, n_tok, w), F32),
        compiler_params=_cparams(("arbitrary", "arbitrary")),
        name="sb_decode",
    )(page_table, qbd, bias_col, k_new, v_new, *([cache_k] * pps), *([cache_v] * pps))


def _ssd_kernel(xbc_ref, z_ref, dt_ref, dtt_ref, cprev_ref, s0_ref, cw_ref, cb_ref, arow_ref, acol_ref,
                dskip_ref, g_ref, exp_ref, y_ref, sout_ref, cbuf, state,
                *, q, n_heads, hdim, n_groups, n_state, conv_w):
    c = pl.program_id(1)
    ssdw = n_heads * hdim
    gw = ssdw // n_groups
    heads_per_group = n_heads // n_groups

    @pl.when(c == 0)
    def _():
        cbuf[0:CONV_HALO, :] = cprev_ref[...]
        state[...] = s0_ref[...].T

    cbuf[CONV_HALO:CONV_HALO + q, :] = xbc_ref[...]
    u = cb_ref[...]
    for w in range(conv_w):
        u = u + cw_ref[w:w + 1, :] * cbuf[pl.ds(CONV_HALO - (conv_w - 1) + w, q), :]
    u = _silu(u)
    cbuf[0:CONV_HALO, :] = cbuf[q:q + CONV_HALO, :]
    xs = u[:, 0:ssdw]
    bmat = u[:, ssdw:ssdw + n_groups * n_state]
    cmat = u[:, ssdw + n_groups * n_state:]

    ri = lax.broadcasted_iota(I32, (q, q), 0)
    ci = lax.broadcasted_iota(I32, (q, q), 1)
    low = jnp.where(ci <= ri, 1.0, 0.0).astype(BF16)
    upp = jnp.where(ri <= ci, 1.0, 0.0).astype(BF16)
    dt = dt_ref[...]
    acum = _dot3_right(low, dt * arow_ref[...])
    acum_t = _dot3_left(dtt_ref[...] * acol_ref[...], upp)
    a_last = acum[q - 1:q, :]
    expand = exp_ref[...]
    dt_x = _dot3_left(dt, expand)
    ds_x = _dot3_left(jnp.exp(a_last - acum), expand)
    ea_x = _dot3_left(jnp.exp(acum), expand)
    el_x = _dot3_left(jnp.broadcast_to(jnp.exp(a_last), (SUBLANES, LANES)), expand)[0:1, :]

    xdt = xs * dt_x
    xdt_b = xdt.astype(BF16)
    xst_b = (xdt * ds_x).astype(BF16)
    lane = lax.broadcasted_iota(I32, (q, LANES), 1)
    st = state[...]
    st_b = st.astype(BF16)

    y_diag = []
    y_off = []
    new_state = []
    for g in range(n_groups):
        bg = bmat[:, g * n_state:(g + 1) * n_state].astype(BF16)
        cg = cmat[:, g * n_state:(g + 1) * n_state].astype(BF16)
        cbm = _dot_nt(cg, bg)
        y_off.append(_dot(cg, st_b[:, g * gw:(g + 1) * gw]))
        new_state.append(_dot_tn(bg, xst_b[:, g * gw:(g + 1) * gw]))
        for pr in range(heads_per_group // HEAD_PAIR):
            pair = []
            for j in range(HEAD_PAIR):
                h = g * heads_per_group + pr * HEAD_PAIR + j
                seg = jnp.minimum(acum[:, h:h + 1] - acum_t[h:h + 1, :], 0.0)
                m = jnp.where(ci <= ri, cbm * jnp.exp(seg), 0.0).astype(BF16)
                lo = (h // HEAD_PAIR) * LANES
                pair.append(_dot(m, xdt_b[:, lo:lo + LANES]))
            y_diag.append(jnp.where(lane < hdim, pair[0], pair[1]))

    y = jnp.concatenate(y_diag, axis=-1) + jnp.concatenate(y_off, axis=-1) * ea_x + dskip_ref[...] * xs
    y = y * _silu(z_ref[...])
    normed = []
    for g in range(n_groups):
        yg = y[:, g * gw:(g + 1) * gw]
        normed.append(yg * lax.rsqrt(jnp.mean(yg * yg, axis=-1, keepdims=True) + NORM_EPS))
    y_ref[...] = jnp.concatenate(normed, axis=-1) * g_ref[...]

    st_new = st * el_x + jnp.concatenate(new_state, axis=-1)
    state[...] = st_new

    @pl.when(c == pl.num_programs(1) - 1)
    def _():
        sout_ref[...] = st_new.T


def _ssd(xbc, z, dt, dtt, conv_prev, state0, conv_w, conv_b, a_row, a_col, dskip_row, g_row, expand,
         *, n_heads, hdim, n_groups, n_state):
    b, l, cch = xbc.shape
    q = SSD_CHUNK
    nc = l // q
    ssdw = n_heads * hdim
    kw = conv_w.shape[0]
    full = lambda a: pl.BlockSpec(a.shape, lambda bi, ci: (0,) * a.ndim)
    tok = lambda n: pl.BlockSpec((None, q, n), lambda bi, ci: (bi, ci, 0))
    per_b = lambda r, n: pl.BlockSpec((None, r, n), lambda bi, ci: (bi, 0, 0))
    return pl.pallas_call(
        functools.partial(_ssd_kernel, q=q, n_heads=n_heads, hdim=hdim, n_groups=n_groups, n_state=n_state,
                          conv_w=kw),
        grid=(b, nc),
        in_specs=[tok(cch), tok(ssdw), tok(LANES), pl.BlockSpec((LANES, q), lambda bi, ci: (0, bi * nc + ci)),
                  per_b(CONV_HALO, cch), per_b(ssdw, n_state), full(conv_w), full(conv_b), full(a_row),
                  full(a_col), full(dskip_row), full(g_row), full(expand)],
        out_specs=[tok(ssdw), per_b(ssdw, n_state)],
        out_shape=[jax.ShapeDtypeStruct((b, l, ssdw), F32), jax.ShapeDtypeStruct((b, ssdw, n_state), F32)],
        scratch_shapes=[pltpu.VMEM((q + CONV_HALO, cch), F32), pltpu.VMEM((n_state, ssdw), F32)],
        compiler_params=_cparams(("arbitrary", "arbitrary")),
        name="ssd",
    )(xbc, z, dt, dtt, conv_prev, state0, conv_w, conv_b, a_row, a_col, dskip_row, g_row, expand)


def _post_kernel(x_ref, o_ref, y_ref, gate_ref, sh_ref, sc_ref, gsb_ref, wout_ref, g2_ref, wrh_ref, wrl_ref,
                 br_ref, x1_ref, h2_ref, idx_ref, gt_ref, *, sbw, n_exp):
    o = o_ref[...]
    on = o * lax.rsqrt(jnp.mean(o * o, axis=-1, keepdims=True) + NORM_EPS) * gsb_ref[...]
    out = _dot(on.astype(BF16), wout_ref[0:sbw, :]) + _dot(y_ref[...].astype(BF16), wout_ref[sbw:, :])
    x1 = x_ref[...] + gate_ref[...] * out
    x1_ref[...] = x1
    h2 = x1 * lax.rsqrt(jnp.mean(x1 * x1, axis=-1, keepdims=True) + NORM_EPS) * g2_ref[...]
    h2 = h2 * (1.0 + sc_ref[...]) + sh_ref[...]
    h2_ref[...] = h2

    hh = h2.astype(BF16)
    hl = (h2 - hh.astype(F32)).astype(BF16)
    logits = _dot(hh, wrh_ref[...]) + _dot(hl, wrh_ref[...]) + _dot(hh, wrl_ref[...]) + br_ref[...]
    lane = lax.broadcasted_iota(I32, logits.shape, 1)
    neg = jnp.float32(-jnp.inf)
    lg = jnp.where(lane < n_exp, logits, neg)
    vals, idxs = [], []
    for _ in range(TOP_K):
        m = jnp.max(lg, axis=-1, keepdims=True)
        ix = jnp.min(jnp.where(lg == m, lane, LANES), axis=-1, keepdims=True)
        vals.append(m)
        idxs.append(ix)
        lg = jnp.where(lane == ix, neg, lg)
    es = [jnp.exp(v - vals[0]) for v in vals]
    den = es[0]
    for e in es[1:]:
        den = den + e
    idx_out = jnp.zeros(logits.shape, I32)
    gt_out = jnp.zeros(logits.shape, F32)
    for kk in range(TOP_K):
        idx_out = jnp.where(lane == kk, idxs[kk], idx_out)
        gt_out = jnp.where(lane == kk, es[kk] / den, gt_out)
    idx_ref[...] = idx_out
    gt_ref[...] = gt_out


def _post(x, o_sb, y_ssd, gate, shift, scale, g_sb, w_out, g2, wr_hi, wr_lo, b_router, *, per_row,
          tokens_per_batch, n_exp):
    t, d = x.shape
    sbw = o_sb.shape[1]
    tm = min(PROJ_TM, t)
    if per_row:
        mod_spec = pl.BlockSpec((tm, d), lambda i: (i, 0))
    else:
        tiles_per_batch = tokens_per_batch // tm
        mod_spec = pl.BlockSpec((None, 1, d), lambda i: (i // tiles_per_batch, 0, 0))
    full = lambda a: pl.BlockSpec(a.shape, lambda i: (0,) * a.ndim)
    row = lambda n: pl.BlockSpec((tm, n), lambda i: (i, 0))
    return pl.pallas_call(
        functools.partial(_post_kernel, sbw=sbw, n_exp=n_exp),
        grid=(t // tm,),
        in_specs=[row(d), row(sbw), row(y_ssd.shape[1]), mod_spec, mod_spec, mod_spec, full(g_sb), full(w_out),
                  full(g2), full(wr_hi), full(wr_lo), full(b_router)],
        out_specs=[row(d), row(d), row(LANES), row(LANES)],
        out_shape=[jax.ShapeDtypeStruct((t, d), F32), jax.ShapeDtypeStruct((t, d), F32),
                   jax.ShapeDtypeStruct((t, LANES), I32), jax.ShapeDtypeStruct((t, LANES), F32)],
        compiler_params=_cparams(("arbitrary",)),
        name="post",
    )(x, o_sb, y_ssd, gate, shift, scale, g_sb, w_out, g2, wr_hi, wr_lo, b_router)


def _row_copy(src_hbm, dst, src_row, dst_row, sem):
    return pltpu.make_async_copy(src_hbm.at[pl.ds(src_row, 1)], dst.at[pl.ds(dst_row, 1)], sem)


def _moe_gather_kernel(tok_ref, x_hbm, o_ref, sem, *, bm):
    def issue(r, _):
        _row_copy(x_hbm, o_ref, tok_ref[0, r], r, sem).start()
        return _

    lax.fori_loop(0, bm, issue, 0, unroll=8)

    def drain(r, _):
        _row_copy(x_hbm, o_ref, 0, r, sem).wait()
        return _

    lax.fori_loop(0, bm, drain, 0, unroll=8)


def _moe_gather(x, slot_tok, *, bm):
    nb = slot_tok.shape[0]
    d = x.shape[1]
    return pl.pallas_call(
        functools.partial(_moe_gather_kernel, bm=bm),
        grid=(nb,),
        in_specs=[pl.BlockSpec((None, 1, bm), lambda i: (i, 0, 0), memory_space=pltpu.SMEM),
                  pl.BlockSpec(memory_space=pl.ANY)],
        out_specs=pl.BlockSpec((bm, d), lambda i: (i, 0)),
        out_shape=jax.ShapeDtypeStruct((nb * bm, d), x.dtype),
        scratch_shapes=[pltpu.SemaphoreType.DMA],
        compiler_params=_cparams(("arbitrary",)),
        name="moe_gather",
    )(slot_tok, x)


def _moe_expert_kernel(be_ref, x_ref, w1_ref, b1_ref, w2_ref, b2_ref, o_ref, *, dff):
    h = _dot(x_ref[...].astype(BF16), w1_ref[...]) + b1_ref[...]
    x_glu = jnp.minimum(h[:, 0:dff], SWIGLU_LIMIT)
    x_lin = jnp.clip(h[:, dff:], -SWIGLU_LIMIT, SWIGLU_LIMIT)
    act = x_glu * jax.nn.sigmoid(SWIGLU_ALPHA * x_glu) * (x_lin + 1.0)
    o_ref[...] = _dot(act.astype(BF16), w2_ref[...]) + b2_ref[...]


def _moe_experts(xs, block_exp, w1p, b1p, w2, b2, *, bm):
    n_slots, d = xs.shape
    dff2 = w1p.shape[2]
    nb = n_slots // bm
    grid_spec = pltpu.PrefetchScalarGridSpec(
        num_scalar_prefetch=1,
        grid=(nb,),
        in_specs=[pl.BlockSpec((bm, d), lambda i, be: (i, 0)),
                  pl.BlockSpec((None, d, dff2), lambda i, be: (be[i], 0, 0)),
                  pl.BlockSpec((None, 1, dff2), lambda i, be: (be[i], 0, 0)),
                  pl.BlockSpec((None, dff2 // 2, d), lambda i, be: (be[i], 0, 0)),
                  pl.BlockSpec((None, 1, d), lambda i, be: (be[i], 0, 0))],
        out_specs=pl.BlockSpec((bm, d), lambda i, be: (i, 0)),
    )
    return pl.pallas_call(
        functools.partial(_moe_expert_kernel, dff=dff2 // 2),
        grid_spec=grid_spec,
        out_shape=jax.ShapeDtypeStruct((n_slots, d), F32),
        compiler_params=_cparams(("arbitrary",)),
        name="moe_experts",
    )(block_exp, xs, w1p, b1p, w2, b2)


def _moe_combine_kernel(slot_ref, eo_hbm, x1_ref, gt_ref, gate_ref, sh_ref, sc_ref, gf_ref, y_ref, buf, sem,
                        *, tm):
    n = tm * TOP_K

    def issue(r, _):
        _row_copy(eo_hbm, buf, slot_ref[0, r], r, sem).start()
        return _

    lax.fori_loop(0, n, issue, 0, unroll=8)

    def drain(r, _):
        _row_copy(eo_hbm, buf, 0, r, sem).wait()
        return _

    lax.fori_loop(0, n, drain, 0, unroll=8)

    gt = gt_ref[...]
    ff = gt[:, 0:1] * buf[0:tm, :]
    for kk in range(1, TOP_K):
        ff = ff + gt[:, kk:kk + 1] * buf[kk * tm:(kk + 1) * tm, :]
    x2 = x1_ref[...] + gate_ref[...] * ff
    h = x2 * lax.rsqrt(jnp.mean(x2 * x2, axis=-1, keepdims=True) + NORM_EPS) * gf_ref[...]
    y_ref[...] = h * (1.0 + sc_ref[...]) + sh_ref[...]


def _moe_combine(eo, slots, x1, gates, gate, shift, scale, g_final, *, per_row, tokens_per_batch):
    t, d = x1.shape
    tm = min(COMB_TM, t)
    if per_row:
        mod_spec = pl.BlockSpec((tm, d), lambda i: (i, 0))
    else:
        tiles_per_batch = tokens_per_batch // tm
        mod_spec = pl.BlockSpec((None, 1, d), lambda i: (i // tiles_per_batch, 0, 0))
    row = lambda n: pl.BlockSpec((tm, n), lambda i: (i, 0))
    return pl.pallas_call(
        functools.partial(_moe_combine_kernel, tm=tm),
        grid=(t // tm,),
        in_specs=[pl.BlockSpec((None, 1, TOP_K * tm), lambda i: (i, 0, 0), memory_space=pltpu.SMEM),
                  pl.BlockSpec(memory_space=pl.ANY), row(d), row(LANES), mod_spec, mod_spec, mod_spec,
                  pl.BlockSpec(g_final.shape, lambda i: (0, 0))],
        out_specs=row(d),
        out_shape=jax.ShapeDtypeStruct((t, d), F32),
        scratch_shapes=[pltpu.VMEM((TOP_K * tm, d), F32), pltpu.SemaphoreType.DMA],
        compiler_params=_cparams(("arbitrary",)),
        name="moe_combine",
    )(slots, eo, x1, gates, gate, shift, scale, g_final)


def _moe_plan(top_idx, n_exp, bm):
    n_tok = top_idx.shape[0]
    n_asg = n_tok * TOP_K
    flat_e = top_idx.reshape(-1)
    counts = jnp.sum(flat_e[:, None] == jnp.arange(n_exp, dtype=I32)[None, :], axis=0, dtype=I32)
    padded = (counts + bm - 1) // bm * bm
    pad_end = jnp.cumsum(padded)
    pad_start = pad_end - padded
    start = jnp.cumsum(counts) - counts
    order = jnp.argsort(flat_e, stable=True).astype(I32)
    pos = jnp.argsort(order).astype(I32)
    slot_of = (pad_start[flat_e] + pos - start[flat_e]).reshape(n_tok, TOP_K)
    n_blocks = -(-(n_asg + n_exp * (bm - 1)) // bm)
    block_exp = jnp.minimum(
        jnp.searchsorted(pad_end, jnp.arange(n_blocks, dtype=I32) * bm, side='right'), n_exp - 1).astype(I32)
    slot = jnp.arange(n_blocks * bm, dtype=I32)
    e_s = block_exp[slot // bm]
    off = slot - pad_start[e_s]
    src = order[jnp.clip(start[e_s] + off, 0, n_asg - 1)] // TOP_K
    slot_tok = jnp.where(off < counts[e_s], src, 0)
    return slot_tok.reshape(n_blocks, 1, bm), slot_of, block_exp


def _tile_slots(slot_of, tm):
    t, k = slot_of.shape
    return slot_of.reshape(t // tm, tm, k).transpose(0, 2, 1).reshape(t // tm, 1, k * tm)


def kernel(x_prompt, x_sample, c_prompt, c_sample, cache_k, cache_v, state_ssm, state_conv, page_table, w_ada, b_ada, g_norm1, w_in, conv_w, conv_b, dt_bias, a_log, d_skip, sb_bias, g_sb, g_ssd, w_out, g_norm2, w_router, b_router, w1, b1, w2, b2, w_ada_final, b_ada_final, g_final):
    bp, tp, d = x_prompt.shape
    bs, ts, _ = x_sample.shape
    depth = w_in.shape[0]
    assert depth == 1, "the MoE combine kernel fuses the final norm, so only one layer is supported"
    n_sb_heads, hd = cache_k.shape[3], cache_k.shape[4]
    sbw = n_sb_heads * hd
    page = cache_k.shape[2]
    n_heads, hdim, n_state = state_ssm.shape[2], state_ssm.shape[3], state_ssm.shape[4]
    ssdw = n_heads * hdim
    convc = state_conv.shape[3]
    kw = conv_w.shape[1]
    n_groups = (convc - ssdw) // (2 * n_state)
    n_exp = w_router.shape[2]
    dff = w2.shape[2]
    n_p, n_s = bp * tp, bs * ts

    c_all = jnp.concatenate([c_prompt, c_sample], axis=0)
    pad_r = (-c_all.shape[0]) % SUBLANES
    c_all = jnp.pad(c_all, ((0, pad_r), (0, 0)))
    m_fin = _ada(c_all, w_ada_final, b_ada_final)

    xp = x_prompt.reshape(n_p, d)
    xs = x_sample.reshape(n_s, d)
    qscale = LOG2E / math.sqrt(hd)
    expand = (jnp.arange(LANES)[:, None] == (jnp.arange(ssdw)[None, :] // hdim)).astype(BF16)
    cache_k2 = cache_k.reshape(depth, cache_k.shape[1], page, sbw)
    cache_v2 = cache_v.reshape(depth, cache_v.shape[1], page, sbw)

    def prompt_mod(mm, i):
        return mm[:bp, i * d:(i + 1) * d].reshape(bp, 1, d)

    def sample_mod(mm, i):
        return jnp.repeat(mm[bp:bp + bs, i * d:(i + 1) * d], ts, axis=0)

    outs = {n: [] for n in ("kp", "vp", "sp", "cp", "ks", "vs", "ss", "cs")}
    for l in range(depth):
        m_all = _ada(c_all, w_ada[l], b_ada[l])
        w_main = w_in[l][:, :3 * sbw + ssdw + convc].astype(BF16)
        w_dt = jnp.pad(w_in[l][:, 3 * sbw + ssdw + convc:], ((0, 0), (0, LANES - n_heads))).astype(BF16)
        w_dtt = w_dt.T
        dtb = jnp.pad(dt_bias[l], (0, LANES - n_heads)).reshape(1, LANES)
        a_neg = jnp.pad(-jnp.exp(a_log[l]), (0, LANES - n_heads))
        g1 = g_norm1[l].reshape(1, d)
        proj = functools.partial(_inproj, g=g1, w_main=w_main, w_dt=w_dt, w_dtt=w_dtt, dtb=dtb,
                                 dtbt=dtb.reshape(LANES, 1), sbw=sbw, ssdw=ssdw, convc=convc, qscale=qscale)
        ssd = functools.partial(
            _ssd, conv_w=conv_w[l], conv_b=conv_b[l].reshape(1, convc), a_row=a_neg.reshape(1, LANES),
            a_col=a_neg.reshape(LANES, 1), dskip_row=jnp.repeat(d_skip[l], hdim).reshape(1, ssdw),
            g_row=g_ssd[l].reshape(1, ssdw), expand=expand, n_heads=n_heads, hdim=hdim, n_groups=n_groups,
            n_state=n_state)
        wr = jnp.pad(w_router[l], ((0, 0), (0, LANES - n_exp)))
        wr_hi = wr.astype(BF16)
        wr_lo = (wr - wr_hi.astype(F32)).astype(BF16)
        post = functools.partial(
            _post, g_sb=g_sb[l].reshape(1, sbw), w_out=w_out[l].astype(BF16), g2=g_norm2[l].reshape(1, d),
            wr_hi=wr_hi, wr_lo=wr_lo, b_router=jnp.pad(b_router[l], (0, LANES - n_exp)).reshape(1, LANES),
            n_exp=n_exp)
        bias2 = sb_bias[l] * LOG2E

        q_p, k_p, v_p, kb_p, vb_p, z_p, xbc_p, dt_p, dtt_p = proj(
            xp, prompt_mod(m_all, 0), prompt_mod(m_all, 1), per_row=False, tokens_per_batch=tp)
        o_p = _sb_prompt(q_p.reshape(bp, tp, sbw), kb_p.reshape(bp, tp, sbw), vb_p.reshape(bp, tp, sbw), bias2,
                         hd=hd)
        y_p, s_p = ssd(xbc_p.reshape(bp, tp, convc), z_p.reshape(bp, tp, ssdw), dt_p.reshape(bp, tp, LANES),
                       dtt_p, jnp.zeros((bp, CONV_HALO, convc), F32), jnp.zeros((bp, ssdw, n_state), F32))
        x1_p, h2_p, idx_p, gt_p = post(xp, o_p.reshape(n_p, sbw), y_p.reshape(n_p, ssdw), prompt_mod(m_all, 2),
                                       prompt_mod(m_all, 3), prompt_mod(m_all, 4), per_row=False,
                                       tokens_per_batch=tp)

        q_s, k_s, v_s, _, _, z_s, xbc_s, dt_s, dtt_s = proj(
            xs, sample_mod(m_all, 0), sample_mod(m_all, 1), per_row=True, tokens_per_batch=ts)
        head_of_lane = jnp.arange(sbw) // hd
        qbd = jnp.where(head_of_lane[None, None, None, :] == jnp.arange(n_sb_heads)[None, None, :, None],
                        q_s.reshape(bs, ts, 1, sbw), jnp.zeros((), BF16)).reshape(bs, ts * n_sb_heads, sbw)
        bias_col = jnp.tile(bias2, ts).reshape(ts * n_sb_heads, 1)
        pad_keys = lambda a: jnp.pad(a.reshape(bs, ts, sbw), ((0, 0), (0, page - ts), (0, 0)))
        o_s = _sb_decode(qbd, bias_col, pad_keys(k_s), pad_keys(v_s), cache_k2[l], cache_v2[l], page_table,
                         n_heads=n_sb_heads, hd=hd)
        pad_tok = lambda a, n: jnp.pad(a.reshape(bs, ts, n), ((0, 0), (0, SSD_CHUNK - ts), (0, 0)))
        dtt_s_pad = jnp.pad(dtt_s.reshape(LANES, bs, ts), ((0, 0), (0, 0), (0, SSD_CHUNK - ts))).reshape(
            LANES, bs * SSD_CHUNK)
        cprev_s = jnp.pad(state_conv[l], ((0, 0), (CONV_HALO - (kw - 1), 0), (0, 0)))
        y_s, s_s = ssd(pad_tok(xbc_s, convc), pad_tok(z_s, ssdw), pad_tok(dt_s, LANES), dtt_s_pad, cprev_s,
                       state_ssm[l].reshape(bs, ssdw, n_state))
        y_s = y_s[:, :ts].reshape(n_s, ssdw)
        x1_s, h2_s, idx_s, gt_s = post(xs, o_s.reshape(n_s, sbw), y_s, sample_mod(m_all, 2), sample_mod(m_all, 3),
                                       sample_mod(m_all, 4), per_row=True, tokens_per_batch=ts)

        outs["kp"].append(k_p.reshape(bp, tp, n_sb_heads, hd))
        outs["vp"].append(v_p.reshape(bp, tp, n_sb_heads, hd))
        outs["sp"].append(s_p.reshape(bp, n_heads, hdim, n_state).astype(state_ssm.dtype))
        outs["cp"].append(xbc_p.reshape(bp, tp, convc)[:, tp - (kw - 1):])
        outs["ks"].append(k_s.reshape(bs, ts, n_sb_heads, hd))
        outs["vs"].append(v_s.reshape(bs, ts, n_sb_heads, hd))
        outs["ss"].append(s_s.reshape(bs, n_heads, hdim, n_state).astype(state_ssm.dtype))
        full_s = jnp.concatenate([state_conv[l], xbc_s.reshape(bs, ts, convc)], axis=1)
        outs["cs"].append(full_s[:, full_s.shape[1] - (kw - 1):])

        h2 = jnp.concatenate([h2_p, h2_s], axis=0)
        top_idx = jnp.concatenate([idx_p[:, :TOP_K], idx_s[:, :TOP_K]], axis=0)
        slot_tok, slot_of, block_exp = _moe_plan(top_idx, n_exp, MOE_BM)
        w1p = jnp.concatenate([w1[l][:, :, 0::2], w1[l][:, :, 1::2]], axis=-1).astype(BF16)
        b1p = jnp.concatenate([b1[l][:, 0::2], b1[l][:, 1::2]], axis=-1).reshape(n_exp, 1, 2 * dff)
        x_slots = _moe_gather(h2, slot_tok, bm=MOE_BM)
        eo = _moe_experts(x_slots, block_exp, w1p, b1p, w2[l].astype(BF16), b2[l].reshape(n_exp, 1, d), bm=MOE_BM)
        gf = g_final.reshape(1, d)
        tmp = min(COMB_TM, n_p)
        tms = min(COMB_TM, n_s)
        xp = _moe_combine(eo, _tile_slots(slot_of[:n_p], tmp), x1_p, gt_p, prompt_mod(m_all, 5),
                          prompt_mod(m_fin, 0), prompt_mod(m_fin, 1), gf, per_row=False, tokens_per_batch=tp)
        xs = _moe_combine(eo, _tile_slots(slot_of[n_p:], tms), x1_s, gt_s, sample_mod(m_all, 5),
                          sample_mod(m_fin, 0), sample_mod(m_fin, 1), gf, per_row=True, tokens_per_batch=ts)

    st = lambda n: jnp.stack(outs[n])
    return (xp.reshape(bp, tp, d), xs.reshape(bs, ts, d), st("kp"), st("vp"), st("sp"), st("cp"),
            st("ks"), st("vs"), st("ss"), st("cs"))
```

```python
import functools
import math

import jax
import jax.numpy as jnp
from jax import lax
from jax.experimental import pallas as pl
from jax.experimental.pallas import tpu as pltpu

F32 = jnp.float32
BF16 = jnp.bfloat16
I32 = jnp.int32

NORM_EPS = 1e-6
N_MOD = 6
TOP_K = 4
SWIGLU_ALPHA = 1.702
SWIGLU_LIMIT = 7.0
LOG2E = 1.4426950408889634

LANES = 128
SUBLANES = 8
HEAD_PAIR = 2
SSD_CHUNK = 128
CONV_HALO = SUBLANES
ATTN_TQ = 256
ATTN_UNROLL = 4
PROJ_TM = 512
MOE_BM = 256
COMB_TM = 128
DEC_PAGES_PER_STEP = 16
VMEM_LIMIT = 56 * 1024 * 1024


def _cparams(sem):
    return pltpu.CompilerParams(dimension_semantics=sem, vmem_limit_bytes=VMEM_LIMIT)


def _silu(x):
    return x * jax.nn.sigmoid(x)


def _softplus(x):
    return jnp.maximum(x, 0.0) + jnp.log1p(jnp.exp(-jnp.abs(x)))


def _split3(x):
    hi = x.astype(BF16)
    r1 = x - hi.astype(F32)
    mid = r1.astype(BF16)
    lo = (r1 - mid.astype(F32)).astype(BF16)
    return hi, mid, lo


def _dot(a, b):
    return jnp.dot(a, b, preferred_element_type=F32)


def _dot_nt(a, b):
    return lax.dot_general(a, b, (((1,), (1,)), ((), ())), preferred_element_type=F32)


def _dot_tn(a, b):
    return lax.dot_general(a, b, (((0,), (0,)), ((), ())), preferred_element_type=F32)


def _dot3_left(x, m_bf16):
    hi, mid, lo = _split3(x)
    return _dot(hi, m_bf16) + _dot(mid, m_bf16) + _dot(lo, m_bf16)


def _dot3_right(m_bf16, x):
    hi, mid, lo = _split3(x)
    return _dot(m_bf16, hi) + _dot(m_bf16, mid) + _dot(m_bf16, lo)


def _ada_kernel(c_ref, w_ref, b_ref, o_ref):
    s = _silu(c_ref[...])
    o_ref[...] = _dot(s.astype(BF16), w_ref[...].astype(BF16)) + b_ref[...]


def _ada(c, w, b):
    r, d = c.shape
    m = w.shape[1]
    tn = 1024
    return pl.pallas_call(
        _ada_kernel,
        grid=(m // tn,),
        in_specs=[pl.BlockSpec((r, d), lambda j: (0, 0)),
                  pl.BlockSpec((d, tn), lambda j: (0, j)),
                  pl.BlockSpec((1, tn), lambda j: (0, j))],
        out_specs=pl.BlockSpec((r, tn), lambda j: (0, j)),
        out_shape=jax.ShapeDtypeStruct((r, m), F32),
        compiler_params=_cparams(("arbitrary",)),
        name="ada",
    )(c, w, b.reshape(1, m))


def _inproj_kernel(x_ref, sh_ref, sc_ref, g_ref, w_ref, wdt_ref, wdtt_ref, dtb_ref, dtbt_ref,
                   q_ref, k_ref, v_ref, kb_ref, vb_ref, z_ref, xbc_ref, dt_ref, dtt_ref,
                   *, sbw, ssdw, convc, qscale):
    x = x_ref[...]
    h = x * lax.rsqrt(jnp.mean(x * x, axis=-1, keepdims=True) + NORM_EPS) * g_ref[...]
    h = h * (1.0 + sc_ref[...]) + sh_ref[...]
    hb = h.astype(BF16)
    q = _dot(hb, w_ref[:, 0:sbw])
    q_ref[...] = (q * qscale).astype(BF16)
    k = _dot(hb, w_ref[:, sbw:2 * sbw])
    k_ref[...] = k
    kb_ref[...] = k.astype(BF16)
    v = _dot(hb, w_ref[:, 2 * sbw:3 * sbw])
    v_ref[...] = v
    vb_ref[...] = v.astype(BF16)
    o = 3 * sbw
    z_ref[...] = _dot(hb, w_ref[:, o:o + ssdw])
    xbc_ref[...] = _dot(hb, w_ref[:, o + ssdw:o + ssdw + convc])
    dt_ref[...] = _softplus(_dot(hb, wdt_ref[...]) + dtb_ref[...])
    dtt_ref[...] = _softplus(_dot_nt(wdtt_ref[...], hb) + dtbt_ref[...])


def _inproj(x, shift, scale, g, w_main, w_dt, w_dtt, dtb, dtbt, *, per_row, tokens_per_batch,
            sbw, ssdw, convc, qscale):
    t, d = x.shape
    tm = min(PROJ_TM, t)
    grid = (t // tm,)
    if per_row:
        mod_spec = pl.BlockSpec((tm, d), lambda i: (i, 0))
    else:
        tiles_per_batch = tokens_per_batch // tm
        mod_spec = pl.BlockSpec((None, 1, d), lambda i: (i // tiles_per_batch, 0, 0))
    full = lambda a: pl.BlockSpec(a.shape, lambda i: (0,) * a.ndim)
    row = lambda n: pl.BlockSpec((tm, n), lambda i: (i, 0))
    out_shape = [
        jax.ShapeDtypeStruct((t, sbw), BF16),
        jax.ShapeDtypeStruct((t, sbw), F32),
        jax.ShapeDtypeStruct((t, sbw), F32),
        jax.ShapeDtypeStruct((t, sbw), BF16),
        jax.ShapeDtypeStruct((t, sbw), BF16),
        jax.ShapeDtypeStruct((t, ssdw), F32),
        jax.ShapeDtypeStruct((t, convc), F32),
        jax.ShapeDtypeStruct((t, LANES), F32),
        jax.ShapeDtypeStruct((LANES, t), F32),
    ]
    out_specs = [row(sbw), row(sbw), row(sbw), row(sbw), row(sbw), row(ssdw), row(convc), row(LANES),
                 pl.BlockSpec((LANES, tm), lambda i: (0, i))]
    return pl.pallas_call(
        functools.partial(_inproj_kernel, sbw=sbw, ssdw=ssdw, convc=convc, qscale=qscale),
        grid=grid,
        in_specs=[row(d), mod_spec, mod_spec, full(g), full(w_main), full(w_dt), full(w_dtt), full(dtb),
                  full(dtbt)],
        out_specs=out_specs,
        out_shape=out_shape,
        compiler_params=_cparams(("arbitrary",)),
        name="inproj",
    )(x, shift, scale, g, w_main, w_dt, w_dtt, dtb, dtbt)


def _sb_tile(qm, kb, vb, tri, bias, mask):
    s = _dot_nt(qm, kb) + bias
    e = jnp.exp2(-jnp.abs(s))
    l2 = jnp.log2(1.0 + e)
    ls = jnp.minimum(s, 0.0) - l2
    lr = ls - s
    if mask is not None:
        lr = jnp.where(mask, lr, 0.0)
    later = _dot(lr.astype(BF16), tri)
    w = jnp.exp2(ls + later)
    if mask is not None:
        w = jnp.where(mask, w, 0.0)
    tot = jnp.sum(lr, axis=-1, keepdims=True)
    return _dot(w.astype(BF16), vb), tot


def _sb_blocks(qm, k_blocks, v_blocks, tri, bias, cs):
    r = qm.shape[0]
    s = jnp.concatenate([_dot_nt(qm, kb) + bias for kb in k_blocks], axis=0)
    e = jnp.exp2(-jnp.abs(s))
    l2 = jnp.log2(1.0 + e)
    ls = jnp.minimum(s, 0.0) - l2
    lr = ls - s
    w = jnp.exp2(ls + _dot(lr.astype(BF16), tri))
    tot = jnp.sum(lr, axis=-1, keepdims=True)
    acc = None
    for p, vb in enumerate(v_blocks):
        wp = (w[p * r:(p + 1) * r] * jnp.exp2(cs)).astype(BF16)
        pv = _dot(wp, vb)
        acc = pv if acc is None else acc + pv
        cs = cs + tot[p * r:(p + 1) * r]
    return acc, cs


def _strict_upper_ones(n):
    r = lax.broadcasted_iota(I32, (n, n), 0)
    c = lax.broadcasted_iota(I32, (n, n), 1)
    return jnp.where(r > c, 1.0, 0.0).astype(BF16)


def _sb_prompt_kernel(bias_ref, q_ref, k_ref, v_ref, o_ref, *, tq, hd):
    hp = pl.program_id(1)
    qi = pl.program_id(2)
    lane = lax.broadcasted_iota(I32, (tq, LANES), 1)
    r = lax.broadcasted_iota(I32, (tq, tq), 0)
    c = lax.broadcasted_iota(I32, (tq, tq), 1)
    diag_mask = c < r
    tri = _strict_upper_ones(tq)
    q = q_ref[...]
    qms = [jnp.where((lane >= hd * j) & (lane < hd * (j + 1)), q, jnp.zeros_like(q)) for j in range(HEAD_PAIR)]
    biases = [bias_ref[hp * HEAD_PAIR + j] for j in range(HEAD_PAIR)]

    start0 = pl.multiple_of(qi * tq, tq)
    kd = k_ref[pl.ds(start0, tq), :]
    vd = v_ref[pl.ds(start0, tq), :]
    carry = []
    for j in range(HEAD_PAIR):
        pv, tot = _sb_tile(qms[j], kd, vd, tri, biases[j], diag_mask)
        carry += [pv, tot]

    def step(carry, blk):
        start = pl.multiple_of(blk * tq, tq)
        kb = k_ref[pl.ds(start, tq), :]
        vb = v_ref[pl.ds(start, tq), :]
        new = []
        for j in range(HEAD_PAIR):
            acc, cs = carry[2 * j], carry[2 * j + 1]
            pv, tot = _sb_tile(qms[j], kb, vb, tri, biases[j], None)
            new += [acc + jnp.exp2(cs) * pv, cs + tot]
        return tuple(new)

    def body_unrolled(i, carry):
        for u in range(ATTN_UNROLL):
            carry = step(carry, qi - 1 - (i * ATTN_UNROLL + u))
        return carry

    carry = lax.fori_loop(0, qi // ATTN_UNROLL, body_unrolled, tuple(carry))
    rem = qi % ATTN_UNROLL
    carry = lax.fori_loop(0, rem, lambda i, carry: step(carry, rem - 1 - i), carry)
    o_ref[...] = jnp.where(lane < hd, carry[0], carry[2])


def _sb_prompt(q, kb, vb, bias2, *, hd):
    b, t, w = q.shape
    tq = min(ATTN_TQ, t)
    n_pairs = w // LANES
    return pl.pallas_call(
        functools.partial(_sb_prompt_kernel, tq=tq, hd=hd),
        grid=(b, n_pairs, t // tq),
        in_specs=[pl.BlockSpec(memory_space=pltpu.SMEM),
                  pl.BlockSpec((None, tq, LANES), lambda bi, hp, qi: (bi, qi, hp)),
                  pl.BlockSpec((None, t, LANES), lambda bi, hp, qi: (bi, 0, hp)),
                  pl.BlockSpec((None, t, LANES), lambda bi, hp, qi: (bi, 0, hp))],
        out_specs=pl.BlockSpec((None, tq, LANES), lambda bi, hp, qi: (bi, qi, hp)),
        out_shape=jax.ShapeDtypeStruct((b, t, w), F32),
        compiler_params=_cparams(("arbitrary", "arbitrary", "arbitrary")),
        name="sb_prompt",
    )(bias2, q, kb, vb)


def _sb_decode_kernel(pt_ref, qbd_ref, bias_ref, kn_ref, vn_ref, *refs, n_pages_step, n_heads, hd, page):
    k_refs = refs[:n_pages_step]
    v_refs = refs[n_pages_step:2 * n_pages_step]
    o_ref = refs[2 * n_pages_step]
    acc_ref, c_ref = refs[2 * n_pages_step + 1:]
    g = pl.program_id(1)
    n_rows = qbd_ref.shape[0]
    qbd = qbd_ref[...]
    bias = bias_ref[...]
    tri = _strict_upper_ones(page)

    @pl.when(g == 0)
    def _():
        rowq = lax.broadcasted_iota(I32, (n_rows, page), 0) // n_heads
        keyi = lax.broadcasted_iota(I32, (n_rows, page), 1)
        pv, tot = _sb_tile(qbd, kn_ref[...].astype(BF16), vn_ref[...].astype(BF16), tri, bias, keyi < rowq)
        acc_ref[...] = pv
        c_ref[...] = tot

    pv, cs = _sb_blocks(qbd, [k[...].astype(BF16) for k in k_refs], [v[...].astype(BF16) for v in v_refs], tri,
                        bias, c_ref[...])
    acc = acc_ref[...] + pv
    acc_ref[...] = acc
    c_ref[...] = cs

    @pl.when(g == pl.num_programs(1) - 1)
    def _():
        w = acc.shape[1]
        hr = lax.broadcasted_iota(I32, (n_rows, w), 0) % n_heads
        hl = lax.broadcasted_iota(I32, (n_rows, w), 1) // hd
        om = jnp.where(hr == hl, acc, 0.0)
        o_ref[...] = jnp.sum(om.reshape(n_rows // n_heads, n_heads, w), axis=1)


def _sb_decode(qbd, bias_col, k_new, v_new, cache_k, cache_v, page_table, *, n_heads, hd):
    s, n_rows, w = qbd.shape
    page = cache_k.shape[1]
    n_pages = page_table.shape[1]
    pps = min(DEC_PAGES_PER_STEP, n_pages)
    n_groups = n_pages // pps
    n_tok = n_rows // n_heads

    def page_spec(j):
        return pl.BlockSpec((None, page, w), lambda si, g, pt: (pt[si, n_pages - 1 - (g * pps + j)], 0, 0))

    seq3 = lambda r: pl.BlockSpec((None, r, w), lambda si, g, pt: (si, 0, 0))
    grid_spec = pltpu.PrefetchScalarGridSpec(
        num_scalar_prefetch=1,
        grid=(s, n_groups),
        in_specs=[seq3(n_rows), pl.BlockSpec((n_rows, 1), lambda si, g, pt: (0, 0)), seq3(page), seq3(page)]
        + [page_spec(j) for j in range(pps)] + [page_spec(j) for j in range(pps)],
        out_specs=seq3(n_tok),
        scratch_shapes=[pltpu.VMEM((n_rows, w), F32), pltpu.VMEM((n_rows, 1), F32)],
    )
    return pl.pallas_call(
        functools.partial(_sb_decode_kernel, n_pages_step=pps, n_heads=n_heads, hd=hd, page=page),
        grid_spec=grid_spec,
        out_shape=jax.ShapeDtypeStruct((s, n_tok, w), F32),
        compiler_params=_cparams(("arbitrary", "arbitrary")),
        name="sb_decode",
    )(page_table, qbd, bias_col, k_new, v_new, *([cache_k] * pps), *([cache_v] * pps))


def _ssd_kernel(xbc_ref, z_ref, dt_ref, dtt_ref, cprev_ref, s0_ref, cw_ref, cb_ref, arow_ref, acol_ref,
                dskip_ref, g_ref, exp_ref, y_ref, sout_ref, cbuf, state,
                *, q, n_heads, hdim, n_groups, n_state, conv_w):
    c = pl.program_id(1)
    ssdw = n_heads * hdim
    gw = ssdw // n_groups
    heads_per_group = n_heads // n_groups

    @pl.when(c == 0)
    def _():
        cbuf[0:CONV_HALO, :] = cprev_ref[...]
        state[...] = s0_ref[...].T

    cbuf[CONV_HALO:CONV_HALO + q, :] = xbc_ref[...]
    u = cb_ref[...]
    for w in range(conv_w):
        u = u + cw_ref[w:w + 1, :] * cbuf[pl.ds(CONV_HALO - (conv_w - 1) + w, q), :]
    u = _silu(u)
    cbuf[0:CONV_HALO, :] = cbuf[q:q + CONV_HALO, :]
    xs = u[:, 0:ssdw]
    bmat = u[:, ssdw:ssdw + n_groups * n_state]
    cmat = u[:, ssdw + n_groups * n_state:]

    ri = lax.broadcasted_iota(I32, (q, q), 0)
    ci = lax.broadcasted_iota(I32, (q, q), 1)
    low = jnp.where(ci <= ri, 1.0, 0.0).astype(BF16)
    upp = jnp.where(ri <= ci, 1.0, 0.0).astype(BF16)
    dt = dt_ref[...]
    acum = _dot3_right(low, dt * arow_ref[...])
    acum_t = _dot3_left(dtt_ref[...] * acol_ref[...], upp)
    a_last = acum[q - 1:q, :]
    expand = exp_ref[...]
    dt_x = _dot3_left(dt, expand)
    ds_x = _dot3_left(jnp.exp(a_last - acum), expand)
    ea_x = _dot3_left(jnp.exp(acum), expand)
    el_x = _dot3_left(jnp.broadcast_to(jnp.exp(a_last), (SUBLANES, LANES)), expand)[0:1, :]

    xdt = xs * dt_x
    xdt_b = xdt.astype(BF16)
    xst_b = (xdt * ds_x).astype(BF16)
    lane = lax.broadcasted_iota(I32, (q, LANES), 1)
    st = state[...]
    st_b = st.astype(BF16)

    y_diag = []
    y_off = []
    new_state = []
    for g in range(n_groups):
        bg = bmat[:, g * n_state:(g + 1) * n_state].astype(BF16)
        cg = cmat[:, g * n_state:(g + 1) * n_state].astype(BF16)
        cbm = _dot_nt(cg, bg)
        y_off.append(_dot(cg, st_b[:, g * gw:(g + 1) * gw]))
        new_state.append(_dot_tn(bg, xst_b[:, g * gw:(g + 1) * gw]))
        for pr in range(heads_per_group // HEAD_PAIR):
            pair = []
            for j in range(HEAD_PAIR):
                h = g * heads_per_group + pr * HEAD_PAIR + j
                seg = jnp.minimum(acum[:, h:h + 1] - acum_t[h:h + 1, :], 0.0)
                m = jnp.where(ci <= ri, cbm * jnp.exp(seg), 0.0).astype(BF16)
                lo = (h // HEAD_PAIR) * LANES
                pair.append(_dot(m, xdt_b[:, lo:lo + LANES]))
            y_diag.append(jnp.where(lane < hdim, pair[0], pair[1]))

    y = jnp.concatenate(y_diag, axis=-1) + jnp.concatenate(y_off, axis=-1) * ea_x + dskip_ref[...] * xs
    y = y * _silu(z_ref[...])
    normed = []
    for g in range(n_groups):
        yg = y[:, g * gw:(g + 1) * gw]
        normed.append(yg * lax.rsqrt(jnp.mean(yg * yg, axis=-1, keepdims=True) + NORM_EPS))
    y_ref[...] = jnp.concatenate(normed, axis=-1) * g_ref[...]

    st_new = st * el_x + jnp.concatenate(new_state, axis=-1)
    state[...] = st_new

    @pl.when(c == pl.num_programs(1) - 1)
    def _():
        sout_ref[...] = st_new.T


def _ssd(xbc, z, dt, dtt, conv_prev, state0, conv_w, conv_b, a_row, a_col, dskip_row, g_row, expand,
         *, n_heads, hdim, n_groups, n_state):
    b, l, cch = xbc.shape
    q = SSD_CHUNK
    nc = l // q
    ssdw = n_heads * hdim
    kw = conv_w.shape[0]
    full = lambda a: pl.BlockSpec(a.shape, lambda bi, ci: (0,) * a.ndim)
    tok = lambda n: pl.BlockSpec((None, q, n), lambda bi, ci: (bi, ci, 0))
    per_b = lambda r, n: pl.BlockSpec((None, r, n), lambda bi, ci: (bi, 0, 0))
    return pl.pallas_call(
        functools.partial(_ssd_kernel, q=q, n_heads=n_heads, hdim=hdim, n_groups=n_groups, n_state=n_state,
                          conv_w=kw),
        grid=(b, nc),
        in_specs=[tok(cch), tok(ssdw), tok(LANES), pl.BlockSpec((LANES, q), lambda bi, ci: (0, bi * nc + ci)),
                  per_b(CONV_HALO, cch), per_b(ssdw, n_state), full(conv_w), full(conv_b), full(a_row),
                  full(a_col), full(dskip_row), full(g_row), full(expand)],
        out_specs=[tok(ssdw), per_b(ssdw, n_state)],
        out_shape=[jax.ShapeDtypeStruct((b, l, ssdw), F32), jax.ShapeDtypeStruct((b, ssdw, n_state), F32)],
        scratch_shapes=[pltpu.VMEM((q + CONV_HALO, cch), F32), pltpu.VMEM((n_state, ssdw), F32)],
        compiler_params=_cparams(("arbitrary", "arbitrary")),
        name="ssd",
    )(xbc, z, dt, dtt, conv_prev, state0, conv_w, conv_b, a_row, a_col, dskip_row, g_row, expand)


def _post_kernel(x_ref, o_ref, y_ref, gate_ref, sh_ref, sc_ref, gsb_ref, wout_ref, g2_ref, wrh_ref, wrl_ref,
                 br_ref, x1_ref, h2_ref, idx_ref, gt_ref, *, sbw, n_exp):
    o = o_ref[...]
    on = o * lax.rsqrt(jnp.mean(o * o, axis=-1, keepdims=True) + NORM_EPS) * gsb_ref[...]
    out = _dot(on.astype(BF16), wout_ref[0:sbw, :]) + _dot(y_ref[...].astype(BF16), wout_ref[sbw:, :])
    x1 = x_ref[...] + gate_ref[...] * out
    x1_ref[...] = x1
    h2 = x1 * lax.rsqrt(jnp.mean(x1 * x1, axis=-1, keepdims=True) + NORM_EPS) * g2_ref[...]
    h2 = h2 * (1.0 + sc_ref[...]) + sh_ref[...]
    h2_ref[...] = h2

    hh = h2.astype(BF16)
    hl = (h2 - hh.astype(F32)).astype(BF16)
    logits = _dot(hh, wrh_ref[...]) + _dot(hl, wrh_ref[...]) + _dot(hh, wrl_ref[...]) + br_ref[...]
    lane = lax.broadcasted_iota(I32, logits.shape, 1)
    neg = jnp.float32(-jnp.inf)
    lg = jnp.where(lane < n_exp, logits, neg)
    vals, idxs = [], []
    for _ in range(TOP_K):
        m = jnp.max(lg, axis=-1, keepdims=True)
        ix = jnp.min(jnp.where(lg == m, lane, LANES), axis=-1, keepdims=True)
        vals.append(m)
        idxs.append(ix)
        lg = jnp.where(lane == ix, neg, lg)
    es = [jnp.exp(v - vals[0]) for v in vals]
    den = es[0]
    for e in es[1:]:
        den = den + e
    idx_out = jnp.zeros(logits.shape, I32)
    gt_out = jnp.zeros(logits.shape, F32)
    for kk in range(TOP_K):
        idx_out = jnp.where(lane == kk, idxs[kk], idx_out)
        gt_out = jnp.where(lane == kk, es[kk] / den, gt_out)
    idx_ref[...] = idx_out
    gt_ref[...] = gt_out


def _post(x, o_sb, y_ssd, gate, shift, scale, g_sb, w_out, g2, wr_hi, wr_lo, b_router, *, per_row,
          tokens_per_batch, n_exp):
    t, d = x.shape
    sbw = o_sb.shape[1]
    tm = min(PROJ_TM, t)
    if per_row:
        mod_spec = pl.BlockSpec((tm, d), lambda i: (i, 0))
    else:
        tiles_per_batch = tokens_per_batch // tm
        mod_spec = pl.BlockSpec((None, 1, d), lambda i: (i // tiles_per_batch, 0, 0))
    full = lambda a: pl.BlockSpec(a.shape, lambda i: (0,) * a.ndim)
    row = lambda n: pl.BlockSpec((tm, n), lambda i: (i, 0))
    return pl.pallas_call(
        functools.partial(_post_kernel, sbw=sbw, n_exp=n_exp),
        grid=(t // tm,),
        in_specs=[row(d), row(sbw), row(y_ssd.shape[1]), mod_spec, mod_spec, mod_spec, full(g_sb), full(w_out),
                  full(g2), full(wr_hi), full(wr_lo), full(b_router)],
        out_specs=[row(d), row(d), row(LANES), row(LANES)],
        out_shape=[jax.ShapeDtypeStruct((t, d), F32), jax.ShapeDtypeStruct((t, d), F32),
                   jax.ShapeDtypeStruct((t, LANES), I32), jax.ShapeDtypeStruct((t, LANES), F32)],
        compiler_params=_cparams(("arbitrary",)),
        name="post",
    )(x, o_sb, y_ssd, gate, shift, scale, g_sb, w_out, g2, wr_hi, wr_lo, b_router)


def _row_copy(src_hbm, dst, src_row, dst_row, sem):
    return pltpu.make_async_copy(src_hbm.at[pl.ds(src_row, 1)], dst.at[pl.ds(dst_row, 1)], sem)


def _moe_gather_kernel(tok_ref, x_hbm, o_ref, sem, *, bm):
    def issue(r, _):
        _row_copy(x_hbm, o_ref, tok_ref[0, r], r, sem).start()
        return _

    lax.fori_loop(0, bm, issue, 0, unroll=8)

    def drain(r, _):
        _row_copy(x_hbm, o_ref, 0, r, sem).wait()
        return _

    lax.fori_loop(0, bm, drain, 0, unroll=8)


def _moe_gather(x, slot_tok, *, bm):
    nb = slot_tok.shape[0]
    d = x.shape[1]
    return pl.pallas_call(
        functools.partial(_moe_gather_kernel, bm=bm),
        grid=(nb,),
        in_specs=[pl.BlockSpec((None, 1, bm), lambda i: (i, 0, 0), memory_space=pltpu.SMEM),
                  pl.BlockSpec(memory_space=pl.ANY)],
        out_specs=pl.BlockSpec((bm, d), lambda i: (i, 0)),
        out_shape=jax.ShapeDtypeStruct((nb * bm, d), x.dtype),
        scratch_shapes=[pltpu.SemaphoreType.DMA],
        compiler_params=_cparams(("arbitrary",)),
        name="moe_gather",
    )(slot_tok, x)


def _moe_expert_kernel(be_ref, x_ref, w1_ref, b1_ref, w2_ref, b2_ref, o_ref, *, dff):
    h = _dot(x_ref[...].astype(BF16), w1_ref[...]) + b1_ref[...]
    x_glu = jnp.minimum(h[:, 0:dff], SWIGLU_LIMIT)
    x_lin = jnp.clip(h[:, dff:], -SWIGLU_LIMIT, SWIGLU_LIMIT)
    act = x_glu * jax.nn.sigmoid(SWIGLU_ALPHA * x_glu) * (x_lin + 1.0)
    o_ref[...] = _dot(act.astype(BF16), w2_ref[...]) + b2_ref[...]


def _moe_experts(xs, block_exp, w1p, b1p, w2, b2, *, bm):
    n_slots, d = xs.shape
    dff2 = w1p.shape[2]
    nb = n_slots // bm
    grid_spec = pltpu.PrefetchScalarGridSpec(
        num_scalar_prefetch=1,
        grid=(nb,),
        in_specs=[pl.BlockSpec((bm, d), lambda i, be: (i, 0)),
                  pl.BlockSpec((None, d, dff2), lambda i, be: (be[i], 0, 0)),
                  pl.BlockSpec((None, 1, dff2), lambda i, be: (be[i], 0, 0)),
                  pl.BlockSpec((None, dff2 // 2, d), lambda i, be: (be[i], 0, 0)),
                  pl.BlockSpec((None, 1, d), lambda i, be: (be[i], 0, 0))],
        out_specs=pl.BlockSpec((bm, d), lambda i, be: (i, 0)),
    )
    return pl.pallas_call(
        functools.partial(_moe_expert_kernel, dff=dff2 // 2),
        grid_spec=grid_spec,
        out_shape=jax.ShapeDtypeStruct((n_slots, d), F32),
        compiler_params=_cparams(("arbitrary",)),
        name="moe_experts",
    )(block_exp, xs, w1p, b1p, w2, b2)


def _moe_combine_kernel(slot_ref, eo_hbm, x1_ref, gt_ref, gate_ref, sh_ref, sc_ref, gf_ref, y_ref, buf, sem,
                        *, tm):
    n = tm * TOP_K

    def issue(r, _):
        _row_copy(eo_hbm, buf, slot_ref[0, r], r, sem).start()
        return _

    lax.fori_loop(0, n, issue, 0, unroll=8)

    def drain(r, _):
        _row_copy(eo_hbm, buf, 0, r, sem).wait()
        return _

    lax.fori_loop(0, n, drain, 0, unroll=8)

    gt = gt_ref[...]
    ff = gt[:, 0:1] * buf[0:tm, :]
    for kk in range(1, TOP_K):
        ff = ff + gt[:, kk:kk + 1] * buf[kk * tm:(kk + 1) * tm, :]
    x2 = x1_ref[...] + gate_ref[...] * ff
    h = x2 * lax.rsqrt(jnp.mean(x2 * x2, axis=-1, keepdims=True) + NORM_EPS) * gf_ref[...]
    y_ref[...] = h * (1.0 + sc_ref[...]) + sh_ref[...]


def _moe_combine(eo, slots, x1, gates, gate, shift, scale, g_final, *, per_row, tokens_per_batch):
    t, d = x1.shape
    tm = min(COMB_TM, t)
    if per_row:
        mod_spec = pl.BlockSpec((tm, d), lambda i: (i, 0))
    else:
        tiles_per_batch = tokens_per_batch // tm
        mod_spec = pl.BlockSpec((None, 1, d), lambda i: (i // tiles_per_batch, 0, 0))
    row = lambda n: pl.BlockSpec((tm, n), lambda i: (i, 0))
    return pl.pallas_call(
        functools.partial(_moe_combine_kernel, tm=tm),
        grid=(t // tm,),
        in_specs=[pl.BlockSpec((None, 1, TOP_K * tm), lambda i: (i, 0, 0), memory_space=pltpu.SMEM),
                  pl.BlockSpec(memory_space=pl.ANY), row(d), row(LANES), mod_spec, mod_spec, mod_spec,
                  pl.BlockSpec(g_final.shape, lambda i: (0, 0))],
        out_specs=row(d),
        out_shape=jax.ShapeDtypeStruct((t, d), F32),
        scratch_shapes=[pltpu.VMEM((TOP_K * tm, d), F32), pltpu.SemaphoreType.DMA],
        compiler_params=_cparams(("arbitrary",)),
        name="moe_combine",
    )(slots, eo, x1, gates, gate, shift, scale, g_final)


def _moe_plan(top_idx, n_exp, bm):
    n_tok = top_idx.shape[0]
    n_asg = n_tok * TOP_K
    flat_e = top_idx.reshape(-1)
    onehot = flat_e[:, None] == jnp.arange(n_exp, dtype=I32)[None, :]
    counts = jnp.sum(onehot, axis=0, dtype=I32)
    padded = (counts + bm - 1) // bm * bm
    pad_end = jnp.cumsum(padded)
    pad_start = pad_end - padded
    start = jnp.cumsum(counts) - counts
    order = jnp.argsort(flat_e, stable=True).astype(I32)
    pos = jnp.argsort(order).astype(I32)
    shift = jnp.sum(jnp.where(onehot, (pad_start - start)[None, :], 0), axis=1, dtype=I32)
    slot_of = (pos + shift).reshape(n_tok, TOP_K)
    n_blocks = -(-(n_asg + n_exp * (bm - 1)) // bm)
    blk_start = jnp.arange(n_blocks, dtype=I32) * bm
    block_exp = jnp.minimum(jnp.sum(pad_end[None, :] <= blk_start[:, None], axis=1, dtype=I32), n_exp - 1)
    off = (blk_start - pad_start[block_exp])[:, None] + jnp.arange(bm, dtype=I32)[None, :]
    src = order[jnp.clip(start[block_exp][:, None] + off, 0, n_asg - 1)] // TOP_K
    slot_tok = jnp.where(off < counts[block_exp][:, None], src, 0)
    return slot_tok.reshape(n_blocks, 1, bm), slot_of, block_exp


def _tile_slots(slot_of, tm):
    t, k = slot_of.shape
    return slot_of.reshape(t // tm, tm, k).transpose(0, 2, 1).reshape(t // tm, 1, k * tm)


def kernel(x_prompt, x_sample, c_prompt, c_sample, cache_k, cache_v, state_ssm, state_conv, page_table, w_ada, b_ada, g_norm1, w_in, conv_w, conv_b, dt_bias, a_log, d_skip, sb_bias, g_sb, g_ssd, w_out, g_norm2, w_router, b_router, w1, b1, w2, b2, w_ada_final, b_ada_final, g_final):
    bp, tp, d = x_prompt.shape
    bs, ts, _ = x_sample.shape
    depth = w_in.shape[0]
    assert depth == 1, "the MoE combine kernel fuses the final norm, so only one layer is supported"
    n_sb_heads, hd = cache_k.shape[3], cache_k.shape[4]
    sbw = n_sb_heads * hd
    page = cache_k.shape[2]
    n_heads, hdim, n_state = state_ssm.shape[2], state_ssm.shape[3], state_ssm.shape[4]
    ssdw = n_heads * hdim
    convc = state_conv.shape[3]
    kw = conv_w.shape[1]
    n_groups = (convc - ssdw) // (2 * n_state)
    n_exp = w_router.shape[2]
    dff = w2.shape[2]
    n_p, n_s = bp * tp, bs * ts

    c_all = jnp.concatenate([c_prompt, c_sample], axis=0)
    pad_r = (-c_all.shape[0]) % SUBLANES
    c_all = jnp.pad(c_all, ((0, pad_r), (0, 0)))
    m_fin = _ada(c_all, w_ada_final, b_ada_final)

    xp = x_prompt.reshape(n_p, d)
    xs = x_sample.reshape(n_s, d)
    qscale = LOG2E / math.sqrt(hd)
    expand = (jnp.arange(LANES)[:, None] == (jnp.arange(ssdw)[None, :] // hdim)).astype(BF16)
    n_pool = cache_k.shape[1]
    cache_k2 = cache_k.reshape(depth * n_pool, page, sbw)
    cache_v2 = cache_v.reshape(depth * n_pool, page, sbw)

    def prompt_mod(mm, i):
        return mm[:bp, i * d:(i + 1) * d].reshape(bp, 1, d)

    def sample_mod(mm, i):
        return jnp.repeat(mm[bp:bp + bs, i * d:(i + 1) * d], ts, axis=0)

    outs = {n: [] for n in ("kp", "vp", "sp", "cp", "ks", "vs", "ss", "cs")}
    for l in range(depth):
        m_all = _ada(c_all, w_ada[l], b_ada[l])
        w_main = w_in[l][:, :3 * sbw + ssdw + convc].astype(BF16)
        w_dt = jnp.pad(w_in[l][:, 3 * sbw + ssdw + convc:], ((0, 0), (0, LANES - n_heads))).astype(BF16)
        w_dtt = w_dt.T
        dtb = jnp.pad(dt_bias[l], (0, LANES - n_heads)).reshape(1, LANES)
        a_neg = jnp.pad(-jnp.exp(a_log[l]), (0, LANES - n_heads))
        g1 = g_norm1[l].reshape(1, d)
        proj = functools.partial(_inproj, g=g1, w_main=w_main, w_dt=w_dt, w_dtt=w_dtt, dtb=dtb,
                                 dtbt=dtb.reshape(LANES, 1), sbw=sbw, ssdw=ssdw, convc=convc, qscale=qscale)
        ssd = functools.partial(
            _ssd, conv_w=conv_w[l], conv_b=conv_b[l].reshape(1, convc), a_row=a_neg.reshape(1, LANES),
            a_col=a_neg.reshape(LANES, 1), dskip_row=jnp.repeat(d_skip[l], hdim).reshape(1, ssdw),
            g_row=g_ssd[l].reshape(1, ssdw), expand=expand, n_heads=n_heads, hdim=hdim, n_groups=n_groups,
            n_state=n_state)
        wr = jnp.pad(w_router[l], ((0, 0), (0, LANES - n_exp)))
        wr_hi = wr.astype(BF16)
        wr_lo = (wr - wr_hi.astype(F32)).astype(BF16)
        post = functools.partial(
            _post, g_sb=g_sb[l].reshape(1, sbw), w_out=w_out[l].astype(BF16), g2=g_norm2[l].reshape(1, d),
            wr_hi=wr_hi, wr_lo=wr_lo, b_router=jnp.pad(b_router[l], (0, LANES - n_exp)).reshape(1, LANES),
            n_exp=n_exp)
        bias2 = sb_bias[l] * LOG2E

        q_p, k_p, v_p, kb_p, vb_p, z_p, xbc_p, dt_p, dtt_p = proj(
            xp, prompt_mod(m_all, 0), prompt_mod(m_all, 1), per_row=False, tokens_per_batch=tp)
        o_p = _sb_prompt(q_p.reshape(bp, tp, sbw), kb_p.reshape(bp, tp, sbw), vb_p.reshape(bp, tp, sbw), bias2,
                         hd=hd)
        y_p, s_p = ssd(xbc_p.reshape(bp, tp, convc), z_p.reshape(bp, tp, ssdw), dt_p.reshape(bp, tp, LANES),
                       dtt_p, jnp.zeros((bp, CONV_HALO, convc), F32), jnp.zeros((bp, ssdw, n_state), F32))
        x1_p, h2_p, idx_p, gt_p = post(xp, o_p.reshape(n_p, sbw), y_p.reshape(n_p, ssdw), prompt_mod(m_all, 2),
                                       prompt_mod(m_all, 3), prompt_mod(m_all, 4), per_row=False,
                                       tokens_per_batch=tp)

        q_s, k_s, v_s, _, _, z_s, xbc_s, dt_s, dtt_s = proj(
            xs, sample_mod(m_all, 0), sample_mod(m_all, 1), per_row=True, tokens_per_batch=ts)
        head_of_lane = jnp.arange(sbw) // hd
        qbd = jnp.where(head_of_lane[None, None, None, :] == jnp.arange(n_sb_heads)[None, None, :, None],
                        q_s.reshape(bs, ts, 1, sbw), jnp.zeros((), BF16)).reshape(bs, ts * n_sb_heads, sbw)
        bias_col = jnp.tile(bias2, ts).reshape(ts * n_sb_heads, 1)
        pad_keys = lambda a: jnp.pad(a.reshape(bs, ts, sbw), ((0, 0), (0, page - ts), (0, 0)))
        o_s = _sb_decode(qbd, bias_col, pad_keys(k_s), pad_keys(v_s), cache_k2, cache_v2,
                         page_table + l * n_pool, n_heads=n_sb_heads, hd=hd)
        pad_tok = lambda a, n: jnp.pad(a.reshape(bs, ts, n), ((0, 0), (0, SSD_CHUNK - ts), (0, 0)))
        dtt_s_pad = jnp.pad(dtt_s.reshape(LANES, bs, ts), ((0, 0), (0, 0), (0, SSD_CHUNK - ts))).reshape(
            LANES, bs * SSD_CHUNK)
        cprev_s = jnp.pad(state_conv[l], ((0, 0), (CONV_HALO - (kw - 1), 0), (0, 0)))
        y_s, s_s = ssd(pad_tok(xbc_s, convc), pad_tok(z_s, ssdw), pad_tok(dt_s, LANES), dtt_s_pad, cprev_s,
                       state_ssm[l].reshape(bs, ssdw, n_state))
        y_s = y_s[:, :ts].reshape(n_s, ssdw)
        x1_s, h2_s, idx_s, gt_s = post(xs, o_s.reshape(n_s, sbw), y_s, sample_mod(m_all, 2), sample_mod(m_all, 3),
                                       sample_mod(m_all, 4), per_row=True, tokens_per_batch=ts)

        outs["kp"].append(k_p.reshape(bp, tp, n_sb_heads, hd))
        outs["vp"].append(v_p.reshape(bp, tp, n_sb_heads, hd))
        outs["sp"].append(s_p.reshape(bp, n_heads, hdim, n_state).astype(state_ssm.dtype))
        outs["cp"].append(xbc_p.reshape(bp, tp, convc)[:, tp - (kw - 1):])
        outs["ks"].append(k_s.reshape(bs, ts, n_sb_heads, hd))
        outs["vs"].append(v_s.reshape(bs, ts, n_sb_heads, hd))
        outs["ss"].append(s_s.reshape(bs, n_heads, hdim, n_state).astype(state_ssm.dtype))
        full_s = jnp.concatenate([state_conv[l], xbc_s.reshape(bs, ts, convc)], axis=1)
        outs["cs"].append(full_s[:, full_s.shape[1] - (kw - 1):])

        h2 = jnp.concatenate([h2_p, h2_s], axis=0)
        top_idx = jnp.concatenate([idx_p[:, :TOP_K], idx_s[:, :TOP_K]], axis=0)
        slot_tok, slot_of, block_exp = _moe_plan(top_idx, n_exp, MOE_BM)
        w1p = jnp.concatenate([w1[l][:, :, 0::2], w1[l][:, :, 1::2]], axis=-1).astype(BF16)
        b1p = jnp.concatenate([b1[l][:, 0::2], b1[l][:, 1::2]], axis=-1).reshape(n_exp, 1, 2 * dff)
        x_slots = _moe_gather(h2, slot_tok, bm=MOE_BM)
        eo = _moe_experts(x_slots, block_exp, w1p, b1p, w2[l].astype(BF16), b2[l].reshape(n_exp, 1, d), bm=MOE_BM)
        gf = g_final.reshape(1, d)
        tmp = min(COMB_TM, n_p)
        tms = min(COMB_TM, n_s)
        xp = _moe_combine(eo, _tile_slots(slot_of[:n_p], tmp), x1_p, gt_p, prompt_mod(m_all, 5),
                          prompt_mod(m_fin, 0), prompt_mod(m_fin, 1), gf, per_row=False, tokens_per_batch=tp)
        xs = _moe_combine(eo, _tile_slots(slot_of[n_p:], tms), x1_s, gt_s, sample_mod(m_all, 5),
                          sample_mod(m_fin, 0), sample_mod(m_fin, 1), gf, per_row=True, tokens_per_batch=ts)

    st = lambda n: jnp.stack(outs[n])
    return (xp.reshape(bp, tp, d), xs.reshape(bs, ts, d), st("kp"), st("vp"), st("sp"), st("cp"),
            st("ks"), st("vs"), st("ss"), st("cs"))
```

```python
import functools
import math

import jax
import jax.numpy as jnp
from jax import lax
from jax.experimental import pallas as pl
from jax.experimental.pallas import tpu as pltpu

F32 = jnp.float32
BF16 = jnp.bfloat16
I32 = jnp.int32

NORM_EPS = 1e-6
N_MOD = 6
TOP_K = 4
SWIGLU_ALPHA = 1.702
SWIGLU_LIMIT = 7.0
LOG2E = 1.4426950408889634

LANES = 128
SUBLANES = 8
HEAD_PAIR = 2
SSD_CHUNK = 128
CONV_HALO = SUBLANES
ATTN_TQ = 256
ATTN_UNROLL = 4
PROJ_TM = 512
MOE_BM = 256
COMB_TM = 128
DEC_PAGES_PER_STEP = 16
VMEM_LIMIT = 56 * 1024 * 1024


def _cparams(sem):
    return pltpu.CompilerParams(dimension_semantics=sem, vmem_limit_bytes=VMEM_LIMIT)


def _silu(x):
    return x * jax.nn.sigmoid(x)


def _softplus(x):
    return jnp.maximum(x, 0.0) + jnp.log1p(jnp.exp(-jnp.abs(x)))


def _split3(x):
    hi = x.astype(BF16)
    r1 = x - hi.astype(F32)
    mid = r1.astype(BF16)
    lo = (r1 - mid.astype(F32)).astype(BF16)
    return hi, mid, lo


def _dot(a, b):
    return jnp.dot(a, b, preferred_element_type=F32)


def _dot_nt(a, b):
    return lax.dot_general(a, b, (((1,), (1,)), ((), ())), preferred_element_type=F32)


def _dot_tn(a, b):
    return lax.dot_general(a, b, (((0,), (0,)), ((), ())), preferred_element_type=F32)


def _dot3_left(x, m_bf16):
    hi, mid, lo = _split3(x)
    return _dot(hi, m_bf16) + _dot(mid, m_bf16) + _dot(lo, m_bf16)


def _dot3_right(m_bf16, x):
    hi, mid, lo = _split3(x)
    return _dot(m_bf16, hi) + _dot(m_bf16, mid) + _dot(m_bf16, lo)


def _ada_kernel(c_ref, w_ref, b_ref, o_ref):
    s = _silu(c_ref[...])
    o_ref[...] = _dot(s.astype(BF16), w_ref[...].astype(BF16)) + b_ref[...]


def _ada(c, w, b):
    r, d = c.shape
    m = w.shape[1]
    tn = 1024
    return pl.pallas_call(
        _ada_kernel,
        grid=(m // tn,),
        in_specs=[pl.BlockSpec((r, d), lambda j: (0, 0)),
                  pl.BlockSpec((d, tn), lambda j: (0, j)),
                  pl.BlockSpec((1, tn), lambda j: (0, j))],
        out_specs=pl.BlockSpec((r, tn), lambda j: (0, j)),
        out_shape=jax.ShapeDtypeStruct((r, m), F32),
        compiler_params=_cparams(("arbitrary",)),
        name="ada",
    )(c, w, b.reshape(1, m))


def _inproj_kernel(x_ref, sh_ref, sc_ref, g_ref, w_ref, wdt_ref, wdtt_ref, dtb_ref, dtbt_ref,
                   q_ref, k_ref, v_ref, kb_ref, vb_ref, z_ref, xbc_ref, dt_ref, dtt_ref,
                   *, sbw, ssdw, convc, qscale):
    x = x_ref[...]
    h = x * lax.rsqrt(jnp.mean(x * x, axis=-1, keepdims=True) + NORM_EPS) * g_ref[...]
    h = h * (1.0 + sc_ref[...]) + sh_ref[...]
    hb = h.astype(BF16)
    q = _dot(hb, w_ref[:, 0:sbw])
    q_ref[...] = (q * qscale).astype(BF16)
    k = _dot(hb, w_ref[:, sbw:2 * sbw])
    k_ref[...] = k
    kb_ref[...] = k.astype(BF16)
    v = _dot(hb, w_ref[:, 2 * sbw:3 * sbw])
    v_ref[...] = v
    vb_ref[...] = v.astype(BF16)
    o = 3 * sbw
    z_ref[...] = _dot(hb, w_ref[:, o:o + ssdw])
    xbc_ref[...] = _dot(hb, w_ref[:, o + ssdw:o + ssdw + convc])
    dt_ref[...] = _softplus(_dot(hb, wdt_ref[...]) + dtb_ref[...])
    dtt_ref[...] = _softplus(_dot_nt(wdtt_ref[...], hb) + dtbt_ref[...])


def _inproj(x, shift, scale, g, w_main, w_dt, w_dtt, dtb, dtbt, *, per_row, tokens_per_batch,
            sbw, ssdw, convc, qscale):
    t, d = x.shape
    tm = min(PROJ_TM, t)
    grid = (t // tm,)
    if per_row:
        mod_spec = pl.BlockSpec((tm, d), lambda i: (i, 0))
    else:
        tiles_per_batch = tokens_per_batch // tm
        mod_spec = pl.BlockSpec((None, 1, d), lambda i: (i // tiles_per_batch, 0, 0))
    full = lambda a: pl.BlockSpec(a.shape, lambda i: (0,) * a.ndim)
    row = lambda n: pl.BlockSpec((tm, n), lambda i: (i, 0))
    out_shape = [
        jax.ShapeDtypeStruct((t, sbw), BF16),
        jax.ShapeDtypeStruct((t, sbw), F32),
        jax.ShapeDtypeStruct((t, sbw), F32),
        jax.ShapeDtypeStruct((t, sbw), BF16),
        jax.ShapeDtypeStruct((t, sbw), BF16),
        jax.ShapeDtypeStruct((t, ssdw), F32),
        jax.ShapeDtypeStruct((t, convc), F32),
        jax.ShapeDtypeStruct((t, LANES), F32),
        jax.ShapeDtypeStruct((LANES, t), F32),
    ]
    out_specs = [row(sbw), row(sbw), row(sbw), row(sbw), row(sbw), row(ssdw), row(convc), row(LANES),
                 pl.BlockSpec((LANES, tm), lambda i: (0, i))]
    return pl.pallas_call(
        functools.partial(_inproj_kernel, sbw=sbw, ssdw=ssdw, convc=convc, qscale=qscale),
        grid=grid,
        in_specs=[row(d), mod_spec, mod_spec, full(g), full(w_main), full(w_dt), full(w_dtt), full(dtb),
                  full(dtbt)],
        out_specs=out_specs,
        out_shape=out_shape,
        compiler_params=_cparams(("arbitrary",)),
        name="inproj",
    )(x, shift, scale, g, w_main, w_dt, w_dtt, dtb, dtbt)


def _sb_tile(qm, kb, vb, tri, bias, mask):
    s = _dot_nt(qm, kb) + bias
    e = jnp.exp2(-jnp.abs(s))
    l2 = jnp.log2(1.0 + e)
    ls = jnp.minimum(s, 0.0) - l2
    lr = ls - s
    if mask is not None:
        lr = jnp.where(mask, lr, 0.0)
    later = _dot(lr.astype(BF16), tri)
    w = jnp.exp2(ls + later)
    if mask is not None:
        w = jnp.where(mask, w, 0.0)
    tot = jnp.sum(lr, axis=-1, keepdims=True)
    return _dot(w.astype(BF16), vb), tot


def _strict_upper_ones(n):
    r = lax.broadcasted_iota(I32, (n, n), 0)
    c = lax.broadcasted_iota(I32, (n, n), 1)
    return jnp.where(r > c, 1.0, 0.0).astype(BF16)


def _sb_prompt_kernel(bias_ref, q_ref, k_ref, v_ref, o_ref, *, tq, hd):
    hp = pl.program_id(1)
    qi = pl.program_id(2)
    lane = lax.broadcasted_iota(I32, (tq, LANES), 1)
    r = lax.broadcasted_iota(I32, (tq, tq), 0)
    c = lax.broadcasted_iota(I32, (tq, tq), 1)
    diag_mask = c < r
    tri = _strict_upper_ones(tq)
    q = q_ref[...]
    qms = [jnp.where((lane >= hd * j) & (lane < hd * (j + 1)), q, jnp.zeros_like(q)) for j in range(HEAD_PAIR)]
    biases = [bias_ref[hp * HEAD_PAIR + j] for j in range(HEAD_PAIR)]

    start0 = pl.multiple_of(qi * tq, tq)
    kd = k_ref[pl.ds(start0, tq), :]
    vd = v_ref[pl.ds(start0, tq), :]
    carry = []
    for j in range(HEAD_PAIR):
        pv, tot = _sb_tile(qms[j], kd, vd, tri, biases[j], diag_mask)
        carry += [pv, tot]

    def step(carry, blk):
        start = pl.multiple_of(blk * tq, tq)
        kb = k_ref[pl.ds(start, tq), :]
        vb = v_ref[pl.ds(start, tq), :]
        new = []
        for j in range(HEAD_PAIR):
            acc, cs = carry[2 * j], carry[2 * j + 1]
            pv, tot = _sb_tile(qms[j], kb, vb, tri, biases[j], None)
            new += [acc + jnp.exp2(cs) * pv, cs + tot]
        return tuple(new)

    def body_unrolled(i, carry):
        for u in range(ATTN_UNROLL):
            carry = step(carry, qi - 1 - (i * ATTN_UNROLL + u))
        return carry

    carry = lax.fori_loop(0, qi // ATTN_UNROLL, body_unrolled, tuple(carry))
    rem = qi % ATTN_UNROLL
    carry = lax.fori_loop(0, rem, lambda i, carry: step(carry, rem - 1 - i), carry)
    o_ref[...] = jnp.where(lane < hd, carry[0], carry[2])


def _sb_prompt(q, kb, vb, bias2, *, hd):
    b, t, w = q.shape
    tq = min(ATTN_TQ, t)
    n_pairs = w // LANES
    return pl.pallas_call(
        functools.partial(_sb_prompt_kernel, tq=tq, hd=hd),
        grid=(b, n_pairs, t // tq),
        in_specs=[pl.BlockSpec(memory_space=pltpu.SMEM),
                  pl.BlockSpec((None, tq, LANES), lambda bi, hp, qi: (bi, qi, hp)),
                  pl.BlockSpec((None, t, LANES), lambda bi, hp, qi: (bi, 0, hp)),
                  pl.BlockSpec((None, t, LANES), lambda bi, hp, qi: (bi, 0, hp))],
        out_specs=pl.BlockSpec((None, tq, LANES), lambda bi, hp, qi: (bi, qi, hp)),
        out_shape=jax.ShapeDtypeStruct((b, t, w), F32),
        compiler_params=_cparams(("arbitrary", "arbitrary", "arbitrary")),
        name="sb_prompt",
    )(bias2, q, kb, vb)


def _decode_blocks(qbd, kt_pages, vt_pages, tri, bias, cs, key_mask):
    r = qbd.shape[0]
    n = len(kt_pages)
    s = jnp.concatenate([_dot(qbd, kt) for kt in kt_pages], axis=0) + jnp.concatenate([bias] * n, axis=0)
    e = jnp.exp2(-jnp.abs(s))
    l2 = jnp.log2(1.0 + e)
    ls = jnp.minimum(s, 0.0) - l2
    lr = ls - s
    if key_mask is not None:
        lr = jnp.where(key_mask, lr, 0.0)
    w = jnp.exp2(ls + _dot(lr.astype(BF16), tri))
    if key_mask is not None:
        w = jnp.where(key_mask, w, 0.0)
    tot = jnp.sum(lr, axis=-1, keepdims=True)
    acc = None
    for p, vt in enumerate(vt_pages):
        wp = (w[p * r:(p + 1) * r] * jnp.exp2(cs)).astype(BF16)
        pv = _dot_nt(wp, vt)
        acc = pv if acc is None else acc + pv
        cs = cs + tot[p * r:(p + 1) * r]
    return acc, cs


def _sb_decode_kernel(pt_ref, qbd_ref, bias_ref, kn_ref, vn_ref, *refs, n_pages_step, n_heads, hd, page):
    k_refs = refs[:n_pages_step]
    v_refs = refs[n_pages_step:2 * n_pages_step]
    o_ref = refs[2 * n_pages_step]
    acc_ref, c_ref = refs[2 * n_pages_step + 1:]
    g = pl.program_id(1)
    n_rows = qbd_ref.shape[0]
    qbd = qbd_ref[...]
    bias = bias_ref[...]
    tri = _strict_upper_ones(page)

    @pl.when(g == 0)
    def _():
        rowq = lax.broadcasted_iota(I32, (n_rows, page), 0) // n_heads
        keyi = lax.broadcasted_iota(I32, (n_rows, page), 1)
        pv, tot = _decode_blocks(qbd, [kn_ref[...].astype(BF16)], [vn_ref[...].astype(BF16)], tri, bias,
                                 jnp.zeros((n_rows, 1), F32), keyi < rowq)
        acc_ref[...] = pv
        c_ref[...] = tot

    pv, cs = _decode_blocks(qbd, [k[...].astype(BF16) for k in k_refs], [v[...].astype(BF16) for v in v_refs], tri,
                            bias, c_ref[...], None)
    acc = acc_ref[...] + pv
    acc_ref[...] = acc
    c_ref[...] = cs

    @pl.when(g == pl.num_programs(1) - 1)
    def _():
        w = acc.shape[1]
        hr = lax.broadcasted_iota(I32, (n_rows, w), 0) % n_heads
        hl = lax.broadcasted_iota(I32, (n_rows, w), 1) // hd
        om = jnp.where(hr == hl, acc, 0.0)
        o_ref[...] = jnp.sum(om.reshape(n_rows // n_heads, n_heads, w), axis=1)


def _sb_decode(qbd, bias_col, kt_new, vt_new, cache_kt, cache_vt, page_table, *, n_heads, hd):
    s, n_rows, w = qbd.shape
    page = cache_kt.shape[2]
    n_pages = page_table.shape[1]
    pps = min(DEC_PAGES_PER_STEP, n_pages)
    n_groups = n_pages // pps
    n_tok = n_rows // n_heads

    def page_spec(j):
        return pl.BlockSpec((None, w, page), lambda si, g, pt: (pt[si, n_pages - 1 - (g * pps + j)], 0, 0))

    seq3 = lambda r, c: pl.BlockSpec((None, r, c), lambda si, g, pt: (si, 0, 0))
    grid_spec = pltpu.PrefetchScalarGridSpec(
        num_scalar_prefetch=1,
        grid=(s, n_groups),
        in_specs=[seq3(n_rows, w), pl.BlockSpec((n_rows, 1), lambda si, g, pt: (0, 0)), seq3(w, page), seq3(w, page)]
        + [page_spec(j) for j in range(pps)] + [page_spec(j) for j in range(pps)],
        out_specs=seq3(n_tok, w),
        scratch_shapes=[pltpu.VMEM((n_rows, w), F32), pltpu.VMEM((n_rows, 1), F32)],
    )
    return pl.pallas_call(
        functools.partial(_sb_decode_kernel, n_pages_step=pps, n_heads=n_heads, hd=hd, page=page),
        grid_spec=grid_spec,
        out_shape=jax.ShapeDtypeStruct((s, n_tok, w), F32),
        compiler_params=_cparams(("arbitrary", "arbitrary")),
        name="sb_decode",
    )(page_table, qbd, bias_col, kt_new, vt_new, *([cache_kt] * pps), *([cache_vt] * pps))


def _ssd_kernel(xbc_ref, z_ref, dt_ref, dtt_ref, cprev_ref, s0_ref, cw_ref, cb_ref, arow_ref, acol_ref,
                dskip_ref, g_ref, exp_ref, y_ref, sout_ref, cbuf, state,
                *, q, n_heads, hdim, n_groups, n_state, conv_w):
    c = pl.program_id(1)
    ssdw = n_heads * hdim
    gw = ssdw // n_groups
    heads_per_group = n_heads // n_groups

    @pl.when(c == 0)
    def _():
        cbuf[0:CONV_HALO, :] = cprev_ref[...]
        state[...] = s0_ref[...].T

    cbuf[CONV_HALO:CONV_HALO + q, :] = xbc_ref[...]
    u = cb_ref[...]
    for w in range(conv_w):
        u = u + cw_ref[w:w + 1, :] * cbuf[pl.ds(CONV_HALO - (conv_w - 1) + w, q), :]
    u = _silu(u)
    cbuf[0:CONV_HALO, :] = cbuf[q:q + CONV_HALO, :]
    xs = u[:, 0:ssdw]
    bmat = u[:, ssdw:ssdw + n_groups * n_state]
    cmat = u[:, ssdw + n_groups * n_state:]

    ri = lax.broadcasted_iota(I32, (q, q), 0)
    ci = lax.broadcasted_iota(I32, (q, q), 1)
    low = jnp.where(ci <= ri, 1.0, 0.0).astype(BF16)
    upp = jnp.where(ri <= ci, 1.0, 0.0).astype(BF16)
    dt = dt_ref[...]
    acum = _dot3_right(low, dt * arow_ref[...])
    acum_t = _dot3_left(dtt_ref[...] * acol_ref[...], upp)
    a_last = acum[q - 1:q, :]
    expand = exp_ref[...]
    dt_x = _dot3_left(dt, expand)
    ds_x = _dot3_left(jnp.exp(a_last - acum), expand)
    ea_x = _dot3_left(jnp.exp(acum), expand)
    el_x = _dot3_left(jnp.broadcast_to(jnp.exp(a_last), (SUBLANES, LANES)), expand)[0:1, :]

    xdt = xs * dt_x
    xdt_b = xdt.astype(BF16)
    xst_b = (xdt * ds_x).astype(BF16)
    lane = lax.broadcasted_iota(I32, (q, LANES), 1)
    st = state[...]
    st_b = st.astype(BF16)

    y_diag = []
    y_off = []
    new_state = []
    for g in range(n_groups):
        bg = bmat[:, g * n_state:(g + 1) * n_state].astype(BF16)
        cg = cmat[:, g * n_state:(g + 1) * n_state].astype(BF16)
        cbm = _dot_nt(cg, bg)
        y_off.append(_dot(cg, st_b[:, g * gw:(g + 1) * gw]))
        new_state.append(_dot_tn(bg, xst_b[:, g * gw:(g + 1) * gw]))
        for pr in range(heads_per_group // HEAD_PAIR):
            pair = []
            for j in range(HEAD_PAIR):
                h = g * heads_per_group + pr * HEAD_PAIR + j
                seg = jnp.minimum(acum[:, h:h + 1] - acum_t[h:h + 1, :], 0.0)
                m = jnp.where(ci <= ri, cbm * jnp.exp(seg), 0.0).astype(BF16)
                lo = (h // HEAD_PAIR) * LANES
                pair.append(_dot(m, xdt_b[:, lo:lo + LANES]))
            y_diag.append(jnp.where(lane < hdim, pair[0], pair[1]))

    y = jnp.concatenate(y_diag, axis=-1) + jnp.concatenate(y_off, axis=-1) * ea_x + dskip_ref[...] * xs
    y = y * _silu(z_ref[...])
    normed = []
    for g in range(n_groups):
        yg = y[:, g * gw:(g + 1) * gw]
        normed.append(yg * lax.rsqrt(jnp.mean(yg * yg, axis=-1, keepdims=True) + NORM_EPS))
    y_ref[...] = jnp.concatenate(normed, axis=-1) * g_ref[...]

    st_new = st * el_x + jnp.concatenate(new_state, axis=-1)
    state[...] = st_new

    @pl.when(c == pl.num_programs(1) - 1)
    def _():
        sout_ref[...] = st_new.T


def _ssd(xbc, z, dt, dtt, conv_prev, state0, conv_w, conv_b, a_row, a_col, dskip_row, g_row, expand,
         *, n_heads, hdim, n_groups, n_state):
    b, l, cch = xbc.shape
    q = SSD_CHUNK
    nc = l // q
    ssdw = n_heads * hdim
    kw = conv_w.shape[0]
    full = lambda a: pl.BlockSpec(a.shape, lambda bi, ci: (0,) * a.ndim)
    tok = lambda n: pl.BlockSpec((None, q, n), lambda bi, ci: (bi, ci, 0))
    per_b = lambda r, n: pl.BlockSpec((None, r, n), lambda bi, ci: (bi, 0, 0))
    return pl.pallas_call(
        functools.partial(_ssd_kernel, q=q, n_heads=n_heads, hdim=hdim, n_groups=n_groups, n_state=n_state,
                          conv_w=kw),
        grid=(b, nc),
        in_specs=[tok(cch), tok(ssdw), tok(LANES), pl.BlockSpec((LANES, q), lambda bi, ci: (0, bi * nc + ci)),
                  per_b(CONV_HALO, cch), per_b(ssdw, n_state), full(conv_w), full(conv_b), full(a_row),
                  full(a_col), full(dskip_row), full(g_row), full(expand)],
        out_specs=[tok(ssdw), per_b(ssdw, n_state)],
        out_shape=[jax.ShapeDtypeStruct((b, l, ssdw), F32), jax.ShapeDtypeStruct((b, ssdw, n_state), F32)],
        scratch_shapes=[pltpu.VMEM((q + CONV_HALO, cch), F32), pltpu.VMEM((n_state, ssdw), F32)],
        compiler_params=_cparams(("arbitrary", "arbitrary")),
        name="ssd",
    )(xbc, z, dt, dtt, conv_prev, state0, conv_w, conv_b, a_row, a_col, dskip_row, g_row, expand)


def _post_kernel(x_ref, o_ref, y_ref, gate_ref, sh_ref, sc_ref, gsb_ref, wout_ref, g2_ref, wrh_ref, wrl_ref,
                 br_ref, x1_ref, h2_ref, idx_ref, gt_ref, *, sbw, n_exp):
    o = o_ref[...]
    on = o * lax.rsqrt(jnp.mean(o * o, axis=-1, keepdims=True) + NORM_EPS) * gsb_ref[...]
    out = _dot(on.astype(BF16), wout_ref[0:sbw, :]) + _dot(y_ref[...].astype(BF16), wout_ref[sbw:, :])
    x1 = x_ref[...] + gate_ref[...] * out
    x1_ref[...] = x1
    h2 = x1 * lax.rsqrt(jnp.mean(x1 * x1, axis=-1, keepdims=True) + NORM_EPS) * g2_ref[...]
    h2 = h2 * (1.0 + sc_ref[...]) + sh_ref[...]
    h2_ref[...] = h2

    hh = h2.astype(BF16)
    hl = (h2 - hh.astype(F32)).astype(BF16)
    logits = _dot(hh, wrh_ref[...]) + _dot(hl, wrh_ref[...]) + _dot(hh, wrl_ref[...]) + br_ref[...]
    lane = lax.broadcasted_iota(I32, logits.shape, 1)
    neg = jnp.float32(-jnp.inf)
    lg = jnp.where(lane < n_exp, logits, neg)
    vals, idxs = [], []
    for _ in range(TOP_K):
        m = jnp.max(lg, axis=-1, keepdims=True)
        ix = jnp.min(jnp.where(lg == m, lane, LANES), axis=-1, keepdims=True)
        vals.append(m)
        idxs.append(ix)
        lg = jnp.where(lane == ix, neg, lg)
    es = [jnp.exp(v - vals[0]) for v in vals]
    den = es[0]
    for e in es[1:]:
        den = den + e
    idx_out = jnp.zeros(logits.shape, I32)
    gt_out = jnp.zeros(logits.shape, F32)
    for kk in range(TOP_K):
        idx_out = jnp.where(lane == kk, idxs[kk], idx_out)
        gt_out = jnp.where(lane == kk, es[kk] / den, gt_out)
    idx_ref[...] = idx_out
    gt_ref[...] = gt_out


def _post(x, o_sb, y_ssd, gate, shift, scale, g_sb, w_out, g2, wr_hi, wr_lo, b_router, *, per_row,
          tokens_per_batch, n_exp):
    t, d = x.shape
    sbw = o_sb.shape[1]
    tm = min(PROJ_TM, t)
    if per_row:
        mod_spec = pl.BlockSpec((tm, d), lambda i: (i, 0))
    else:
        tiles_per_batch = tokens_per_batch // tm
        mod_spec = pl.BlockSpec((None, 1, d), lambda i: (i // tiles_per_batch, 0, 0))
    full = lambda a: pl.BlockSpec(a.shape, lambda i: (0,) * a.ndim)
    row = lambda n: pl.BlockSpec((tm, n), lambda i: (i, 0))
    return pl.pallas_call(
        functools.partial(_post_kernel, sbw=sbw, n_exp=n_exp),
        grid=(t // tm,),
        in_specs=[row(d), row(sbw), row(y_ssd.shape[1]), mod_spec, mod_spec, mod_spec, full(g_sb), full(w_out),
                  full(g2), full(wr_hi), full(wr_lo), full(b_router)],
        out_specs=[row(d), row(d), row(LANES), row(LANES)],
        out_shape=[jax.ShapeDtypeStruct((t, d), F32), jax.ShapeDtypeStruct((t, d), F32),
                   jax.ShapeDtypeStruct((t, LANES), I32), jax.ShapeDtypeStruct((t, LANES), F32)],
        compiler_params=_cparams(("arbitrary",)),
        name="post",
    )(x, o_sb, y_ssd, gate, shift, scale, g_sb, w_out, g2, wr_hi, wr_lo, b_router)


def _row_copy(src_hbm, dst, src_row, dst_row, sem):
    return pltpu.make_async_copy(src_hbm.at[pl.ds(src_row, 1)], dst.at[pl.ds(dst_row, 1)], sem)


def _moe_gather_kernel(tok_ref, x_hbm, o_ref, sem, *, bm):
    def issue(r, _):
        _row_copy(x_hbm, o_ref, tok_ref[0, r], r, sem).start()
        return _

    lax.fori_loop(0, bm, issue, 0, unroll=8)

    def drain(r, _):
        _row_copy(x_hbm, o_ref, 0, r, sem).wait()
        return _

    lax.fori_loop(0, bm, drain, 0, unroll=8)


def _moe_gather(x, slot_tok, *, bm):
    nb = slot_tok.shape[0]
    d = x.shape[1]
    return pl.pallas_call(
        functools.partial(_moe_gather_kernel, bm=bm),
        grid=(nb,),
        in_specs=[pl.BlockSpec((None, 1, bm), lambda i: (i, 0, 0), memory_space=pltpu.SMEM),
                  pl.BlockSpec(memory_space=pl.ANY)],
        out_specs=pl.BlockSpec((bm, d), lambda i: (i, 0)),
        out_shape=jax.ShapeDtypeStruct((nb * bm, d), x.dtype),
        scratch_shapes=[pltpu.SemaphoreType.DMA],
        compiler_params=_cparams(("arbitrary",)),
        name="moe_gather",
    )(slot_tok, x)


def _moe_expert_kernel(be_ref, x_ref, w1_ref, b1_ref, w2_ref, b2_ref, o_ref, *, dff):
    h = _dot(x_ref[...].astype(BF16), w1_ref[...]) + b1_ref[...]
    x_glu = jnp.minimum(h[:, 0:dff], SWIGLU_LIMIT)
    x_lin = jnp.clip(h[:, dff:], -SWIGLU_LIMIT, SWIGLU_LIMIT)
    act = x_glu * jax.nn.sigmoid(SWIGLU_ALPHA * x_glu) * (x_lin + 1.0)
    o_ref[...] = _dot(act.astype(BF16), w2_ref[...]) + b2_ref[...]


def _moe_experts(xs, block_exp, w1p, b1p, w2, b2, *, bm):
    n_slots, d = xs.shape
    dff2 = w1p.shape[2]
    nb = n_slots // bm
    grid_spec = pltpu.PrefetchScalarGridSpec(
        num_scalar_prefetch=1,
        grid=(nb,),
        in_specs=[pl.BlockSpec((bm, d), lambda i, be: (i, 0)),
                  pl.BlockSpec((None, d, dff2), lambda i, be: (be[i], 0, 0)),
                  pl.BlockSpec((None, 1, dff2), lambda i, be: (be[i], 0, 0)),
                  pl.BlockSpec((None, dff2 // 2, d), lambda i, be: (be[i], 0, 0)),
                  pl.BlockSpec((None, 1, d), lambda i, be: (be[i], 0, 0))],
        out_specs=pl.BlockSpec((bm, d), lambda i, be: (i, 0)),
    )
    return pl.pallas_call(
        functools.partial(_moe_expert_kernel, dff=dff2 // 2),
        grid_spec=grid_spec,
        out_shape=jax.ShapeDtypeStruct((n_slots, d), F32),
        compiler_params=_cparams(("arbitrary",)),
        name="moe_experts",
    )(block_exp, xs, w1p, b1p, w2, b2)


def _moe_combine_kernel(slot_ref, eo_hbm, x1_ref, gt_ref, gate_ref, sh_ref, sc_ref, gf_ref, y_ref, buf, sem,
                        *, tm):
    n = tm * TOP_K

    def issue(r, _):
        _row_copy(eo_hbm, buf, slot_ref[0, r], r, sem).start()
        return _

    lax.fori_loop(0, n, issue, 0, unroll=8)

    def drain(r, _):
        _row_copy(eo_hbm, buf, 0, r, sem).wait()
        return _

    lax.fori_loop(0, n, drain, 0, unroll=8)

    gt = gt_ref[...]
    ff = gt[:, 0:1] * buf[0:tm, :]
    for kk in range(1, TOP_K):
        ff = ff + gt[:, kk:kk + 1] * buf[kk * tm:(kk + 1) * tm, :]
    x2 = x1_ref[...] + gate_ref[...] * ff
    h = x2 * lax.rsqrt(jnp.mean(x2 * x2, axis=-1, keepdims=True) + NORM_EPS) * gf_ref[...]
    y_ref[...] = h * (1.0 + sc_ref[...]) + sh_ref[...]


def _moe_combine(eo, slots, x1, gates, gate, shift, scale, g_final, *, per_row, tokens_per_batch):
    t, d = x1.shape
    tm = min(COMB_TM, t)
    if per_row:
        mod_spec = pl.BlockSpec((tm, d), lambda i: (i, 0))
    else:
        tiles_per_batch = tokens_per_batch // tm
        mod_spec = pl.BlockSpec((None, 1, d), lambda i: (i // tiles_per_batch, 0, 0))
    row = lambda n: pl.BlockSpec((tm, n), lambda i: (i, 0))
    return pl.pallas_call(
        functools.partial(_moe_combine_kernel, tm=tm),
        grid=(t // tm,),
        in_specs=[pl.BlockSpec((None, 1, TOP_K * tm), lambda i: (i, 0, 0), memory_space=pltpu.SMEM),
                  pl.BlockSpec(memory_space=pl.ANY), row(d), row(LANES), mod_spec, mod_spec, mod_spec,
                  pl.BlockSpec(g_final.shape, lambda i: (0, 0))],
        out_specs=row(d),
        out_shape=jax.ShapeDtypeStruct((t, d), F32),
        scratch_shapes=[pltpu.VMEM((TOP_K * tm, d), F32), pltpu.SemaphoreType.DMA],
        compiler_params=_cparams(("arbitrary",)),
        name="moe_combine",
    )(slots, eo, x1, gates, gate, shift, scale, g_final)


def _moe_plan(top_idx, n_exp, bm):
    n_tok = top_idx.shape[0]
    n_asg = n_tok * TOP_K
    flat_e = top_idx.reshape(-1)
    onehot = flat_e[:, None] == jnp.arange(n_exp, dtype=I32)[None, :]
    counts = jnp.sum(onehot, axis=0, dtype=I32)
    padded = (counts + bm - 1) // bm * bm
    pad_end = jnp.cumsum(padded)
    pad_start = pad_end - padded
    start = jnp.cumsum(counts) - counts
    order = jnp.argsort(flat_e, stable=True).astype(I32)
    pos = jnp.argsort(order).astype(I32)
    shift = jnp.sum(jnp.where(onehot, (pad_start - start)[None, :], 0), axis=1, dtype=I32)
    slot_of = (pos + shift).reshape(n_tok, TOP_K)
    n_blocks = -(-(n_asg + n_exp * (bm - 1)) // bm)
    blk_start = jnp.arange(n_blocks, dtype=I32) * bm
    block_exp = jnp.minimum(jnp.sum(pad_end[None, :] <= blk_start[:, None], axis=1, dtype=I32), n_exp - 1)
    off = (blk_start - pad_start[block_exp])[:, None] + jnp.arange(bm, dtype=I32)[None, :]
    src = order[jnp.clip(start[block_exp][:, None] + off, 0, n_asg - 1)] // TOP_K
    slot_tok = jnp.where(off < counts[block_exp][:, None], src, 0)
    return slot_tok.reshape(n_blocks, 1, bm), slot_of, block_exp


def _tile_slots(slot_of, tm):
    t, k = slot_of.shape
    return slot_of.reshape(t // tm, tm, k).transpose(0, 2, 1).reshape(t // tm, 1, k * tm)


def kernel(x_prompt, x_sample, c_prompt, c_sample, cache_k, cache_v, state_ssm, state_conv, page_table, w_ada, b_ada, g_norm1, w_in, conv_w, conv_b, dt_bias, a_log, d_skip, sb_bias, g_sb, g_ssd, w_out, g_norm2, w_router, b_router, w1, b1, w2, b2, w_ada_final, b_ada_final, g_final):
    bp, tp, d = x_prompt.shape
    bs, ts, _ = x_sample.shape
    depth = w_in.shape[0]
    assert depth == 1, "the MoE combine kernel fuses the final norm, so only one layer is supported"
    n_sb_heads, hd = cache_k.shape[3], cache_k.shape[4]
    sbw = n_sb_heads * hd
    page = cache_k.shape[2]
    n_heads, hdim, n_state = state_ssm.shape[2], state_ssm.shape[3], state_ssm.shape[4]
    ssdw = n_heads * hdim
    convc = state_conv.shape[3]
    kw = conv_w.shape[1]
    n_groups = (convc - ssdw) // (2 * n_state)
    n_exp = w_router.shape[2]
    dff = w2.shape[2]
    n_p, n_s = bp * tp, bs * ts

    c_all = jnp.concatenate([c_prompt, c_sample], axis=0)
    pad_r = (-c_all.shape[0]) % SUBLANES
    c_all = jnp.pad(c_all, ((0, pad_r), (0, 0)))
    m_fin = _ada(c_all, w_ada_final, b_ada_final)

    xp = x_prompt.reshape(n_p, d)
    xs = x_sample.reshape(n_s, d)
    qscale = LOG2E / math.sqrt(hd)
    expand = (jnp.arange(LANES)[:, None] == (jnp.arange(ssdw)[None, :] // hdim)).astype(BF16)
    n_pool = cache_k.shape[1]
    cache_k2 = cache_k.transpose(0, 1, 3, 4, 2).reshape(depth * n_pool, sbw, page)
    cache_v2 = cache_v.transpose(0, 1, 3, 4, 2).reshape(depth * n_pool, sbw, page)

    def prompt_mod(mm, i):
        return mm[:bp, i * d:(i + 1) * d].reshape(bp, 1, d)

    def sample_mod(mm, i):
        return jnp.repeat(mm[bp:bp + bs, i * d:(i + 1) * d], ts, axis=0)

    outs = {n: [] for n in ("kp", "vp", "sp", "cp", "ks", "vs", "ss", "cs")}
    for l in range(depth):
        m_all = _ada(c_all, w_ada[l], b_ada[l])
        w_main = w_in[l][:, :3 * sbw + ssdw + convc].astype(BF16)
        w_dt = jnp.pad(w_in[l][:, 3 * sbw + ssdw + convc:], ((0, 0), (0, LANES - n_heads))).astype(BF16)
        w_dtt = w_dt.T
        dtb = jnp.pad(dt_bias[l], (0, LANES - n_heads)).reshape(1, LANES)
        a_neg = jnp.pad(-jnp.exp(a_log[l]), (0, LANES - n_heads))
        g1 = g_norm1[l].reshape(1, d)
        proj = functools.partial(_inproj, g=g1, w_main=w_main, w_dt=w_dt, w_dtt=w_dtt, dtb=dtb,
                                 dtbt=dtb.reshape(LANES, 1), sbw=sbw, ssdw=ssdw, convc=convc, qscale=qscale)
        ssd = functools.partial(
            _ssd, conv_w=conv_w[l], conv_b=conv_b[l].reshape(1, convc), a_row=a_neg.reshape(1, LANES),
            a_col=a_neg.reshape(LANES, 1), dskip_row=jnp.repeat(d_skip[l], hdim).reshape(1, ssdw),
            g_row=g_ssd[l].reshape(1, ssdw), expand=expand, n_heads=n_heads, hdim=hdim, n_groups=n_groups,
            n_state=n_state)
        wr = jnp.pad(w_router[l], ((0, 0), (0, LANES - n_exp)))
        wr_hi = wr.astype(BF16)
        wr_lo = (wr - wr_hi.astype(F32)).astype(BF16)
        post = functools.partial(
            _post, g_sb=g_sb[l].reshape(1, sbw), w_out=w_out[l].astype(BF16), g2=g_norm2[l].reshape(1, d),
            wr_hi=wr_hi, wr_lo=wr_lo, b_router=jnp.pad(b_router[l], (0, LANES - n_exp)).reshape(1, LANES),
            n_exp=n_exp)
        bias2 = sb_bias[l] * LOG2E

        q_p, k_p, v_p, kb_p, vb_p, z_p, xbc_p, dt_p, dtt_p = proj(
            xp, prompt_mod(m_all, 0), prompt_mod(m_all, 1), per_row=False, tokens_per_batch=tp)
        o_p = _sb_prompt(q_p.reshape(bp, tp, sbw), kb_p.reshape(bp, tp, sbw), vb_p.reshape(bp, tp, sbw), bias2,
                         hd=hd)
        y_p, s_p = ssd(xbc_p.reshape(bp, tp, convc), z_p.reshape(bp, tp, ssdw), dt_p.reshape(bp, tp, LANES),
                       dtt_p, jnp.zeros((bp, CONV_HALO, convc), F32), jnp.zeros((bp, ssdw, n_state), F32))
        x1_p, h2_p, idx_p, gt_p = post(xp, o_p.reshape(n_p, sbw), y_p.reshape(n_p, ssdw), prompt_mod(m_all, 2),
                                       prompt_mod(m_all, 3), prompt_mod(m_all, 4), per_row=False,
                                       tokens_per_batch=tp)

        q_s, k_s, v_s, _, _, z_s, xbc_s, dt_s, dtt_s = proj(
            xs, sample_mod(m_all, 0), sample_mod(m_all, 1), per_row=True, tokens_per_batch=ts)
        head_of_lane = jnp.arange(sbw) // hd
        qbd = jnp.where(head_of_lane[None, None, None, :] == jnp.arange(n_sb_heads)[None, None, :, None],
                        q_s.reshape(bs, ts, 1, sbw), jnp.zeros((), BF16)).reshape(bs, ts * n_sb_heads, sbw)
        bias_col = jnp.tile(bias2, ts).reshape(ts * n_sb_heads, 1)
        pad_keys = lambda a: jnp.pad(a.reshape(bs, ts, sbw).transpose(0, 2, 1), ((0, 0), (0, 0), (0, page - ts)))
        o_s = _sb_decode(qbd, bias_col, pad_keys(k_s), pad_keys(v_s), cache_k2, cache_v2,
                         page_table + l * n_pool, n_heads=n_sb_heads, hd=hd)
        pad_tok = lambda a, n: jnp.pad(a.reshape(bs, ts, n), ((0, 0), (0, SSD_CHUNK - ts), (0, 0)))
        dtt_s_pad = jnp.pad(dtt_s.reshape(LANES, bs, ts), ((0, 0), (0, 0), (0, SSD_CHUNK - ts))).reshape(
            LANES, bs * SSD_CHUNK)
        cprev_s = jnp.pad(state_conv[l], ((0, 0), (CONV_HALO - (kw - 1), 0), (0, 0)))
        y_s, s_s = ssd(pad_tok(xbc_s, convc), pad_tok(z_s, ssdw), pad_tok(dt_s, LANES), dtt_s_pad, cprev_s,
                       state_ssm[l].reshape(bs, ssdw, n_state))
        y_s = y_s[:, :ts].reshape(n_s, ssdw)
        x1_s, h2_s, idx_s, gt_s = post(xs, o_s.reshape(n_s, sbw), y_s, sample_mod(m_all, 2), sample_mod(m_all, 3),
                                       sample_mod(m_all, 4), per_row=True, tokens_per_batch=ts)

        outs["kp"].append(k_p.reshape(bp, tp, n_sb_heads, hd))
        outs["vp"].append(v_p.reshape(bp, tp, n_sb_heads, hd))
        outs["sp"].append(s_p.reshape(bp, n_heads, hdim, n_state).astype(state_ssm.dtype))
        outs["cp"].append(xbc_p.reshape(bp, tp, convc)[:, tp - (kw - 1):])
        outs["ks"].append(k_s.reshape(bs, ts, n_sb_heads, hd))
        outs["vs"].append(v_s.reshape(bs, ts, n_sb_heads, hd))
        outs["ss"].append(s_s.reshape(bs, n_heads, hdim, n_state).astype(state_ssm.dtype))
        full_s = jnp.concatenate([state_conv[l], xbc_s.reshape(bs, ts, convc)], axis=1)
        outs["cs"].append(full_s[:, full_s.shape[1] - (kw - 1):])

        h2 = jnp.concatenate([h2_p, h2_s], axis=0)
        top_idx = jnp.concatenate([idx_p[:, :TOP_K], idx_s[:, :TOP_K]], axis=0)
        slot_tok, slot_of, block_exp = _moe_plan(top_idx, n_exp, MOE_BM)
        w1p = jnp.concatenate([w1[l][:, :, 0::2], w1[l][:, :, 1::2]], axis=-1).astype(BF16)
        b1p = jnp.concatenate([b1[l][:, 0::2], b1[l][:, 1::2]], axis=-1).reshape(n_exp, 1, 2 * dff)
        x_slots = _moe_gather(h2, slot_tok, bm=MOE_BM)
        eo = _moe_experts(x_slots, block_exp, w1p, b1p, w2[l].astype(BF16), b2[l].reshape(n_exp, 1, d), bm=MOE_BM)
        gf = g_final.reshape(1, d)
        tmp = min(COMB_TM, n_p)
        tms = min(COMB_TM, n_s)
        xp = _moe_combine(eo, _tile_slots(slot_of[:n_p], tmp), x1_p, gt_p, prompt_mod(m_all, 5),
                          prompt_mod(m_fin, 0), prompt_mod(m_fin, 1), gf, per_row=False, tokens_per_batch=tp)
        xs = _moe_combine(eo, _tile_slots(slot_of[n_p:], tms), x1_s, gt_s, sample_mod(m_all, 5),
                          sample_mod(m_fin, 0), sample_mod(m_fin, 1), gf, per_row=True, tokens_per_batch=ts)

    st = lambda n: jnp.stack(outs[n])
    return (xp.reshape(bp, tp, d), xs.reshape(bs, ts, d), st("kp"), st("vp"), st("sp"), st("cp"),
            st("ks"), st("vs"), st("ss"), st("cs"))
```

```python
import functools
import math

import jax
import jax.numpy as jnp
from jax import lax
from jax.experimental import pallas as pl
from jax.experimental.pallas import tpu as pltpu

F32 = jnp.float32
BF16 = jnp.bfloat16
I32 = jnp.int32

NORM_EPS = 1e-6
N_MOD = 6
TOP_K = 4
SWIGLU_ALPHA = 1.702
SWIGLU_LIMIT = 7.0
LOG2E = 1.4426950408889634

LANES = 128
SUBLANES = 8
HEAD_PAIR = 2
SSD_CHUNK = 128
CONV_HALO = SUBLANES
ATTN_TQ = 256
ATTN_UNROLL = 4
PROJ_TM = 512
MOE_BM = 256
COMB_TM = 128
DEC_PAGES_PER_STEP = 16
VMEM_LIMIT = 56 * 1024 * 1024


def _cparams(sem):
    return pltpu.CompilerParams(dimension_semantics=sem, vmem_limit_bytes=VMEM_LIMIT)


def _silu(x):
    return x * jax.nn.sigmoid(x)


def _softplus(x):
    return jnp.maximum(x, 0.0) + jnp.log1p(jnp.exp(-jnp.abs(x)))


def _split3(x):
    hi = x.astype(BF16)
    r1 = x - hi.astype(F32)
    mid = r1.astype(BF16)
    lo = (r1 - mid.astype(F32)).astype(BF16)
    return hi, mid, lo


def _dot(a, b):
    return jnp.dot(a, b, preferred_element_type=F32)


def _dot_nt(a, b):
    return lax.dot_general(a, b, (((1,), (1,)), ((), ())), preferred_element_type=F32)


def _dot_tn(a, b):
    return lax.dot_general(a, b, (((0,), (0,)), ((), ())), preferred_element_type=F32)


def _dot3_left(x, m_bf16):
    hi, mid, lo = _split3(x)
    return _dot(hi, m_bf16) + _dot(mid, m_bf16) + _dot(lo, m_bf16)


def _dot3_right(m_bf16, x):
    hi, mid, lo = _split3(x)
    return _dot(m_bf16, hi) + _dot(m_bf16, mid) + _dot(m_bf16, lo)


def _ada_kernel(c_ref, w_ref, b_ref, o_ref):
    s = _silu(c_ref[...])
    o_ref[...] = _dot(s.astype(BF16), w_ref[...].astype(BF16)) + b_ref[...]


def _ada(c, w, b):
    r, d = c.shape
    m = w.shape[1]
    tn = 1024
    return pl.pallas_call(
        _ada_kernel,
        grid=(m // tn,),
        in_specs=[pl.BlockSpec((r, d), lambda j: (0, 0)),
                  pl.BlockSpec((d, tn), lambda j: (0, j)),
                  pl.BlockSpec((1, tn), lambda j: (0, j))],
        out_specs=pl.BlockSpec((r, tn), lambda j: (0, j)),
        out_shape=jax.ShapeDtypeStruct((r, m), F32),
        compiler_params=_cparams(("arbitrary",)),
        name="ada",
    )(c, w, b.reshape(1, m))


def _inproj_kernel(x_ref, sh_ref, sc_ref, g_ref, w_ref, wdt_ref, wdtt_ref, dtb_ref, dtbt_ref,
                   q_ref, k_ref, v_ref, kb_ref, vb_ref, z_ref, xbc_ref, dt_ref, dtt_ref,
                   *, sbw, ssdw, convc, qscale):
    x = x_ref[...]
    h = x * lax.rsqrt(jnp.mean(x * x, axis=-1, keepdims=True) + NORM_EPS) * g_ref[...]
    h = h * (1.0 + sc_ref[...]) + sh_ref[...]
    hb = h.astype(BF16)
    q = _dot(hb, w_ref[:, 0:sbw])
    q_ref[...] = (q * qscale).astype(BF16)
    k = _dot(hb, w_ref[:, sbw:2 * sbw])
    k_ref[...] = k
    kb_ref[...] = k.astype(BF16)
    v = _dot(hb, w_ref[:, 2 * sbw:3 * sbw])
    v_ref[...] = v
    vb_ref[...] = v.astype(BF16)
    o = 3 * sbw
    z_ref[...] = _dot(hb, w_ref[:, o:o + ssdw])
    xbc_ref[...] = _dot(hb, w_ref[:, o + ssdw:o + ssdw + convc])
    dt_ref[...] = _softplus(_dot(hb, wdt_ref[...]) + dtb_ref[...])
    dtt_ref[...] = _softplus(_dot_nt(wdtt_ref[...], hb) + dtbt_ref[...])


def _inproj(x, shift, scale, g, w_main, w_dt, w_dtt, dtb, dtbt, *, per_row, tokens_per_batch,
            sbw, ssdw, convc, qscale):
    t, d = x.shape
    tm = min(PROJ_TM, t)
    grid = (t // tm,)
    if per_row:
        mod_spec = pl.BlockSpec((tm, d), lambda i: (i, 0))
    else:
        tiles_per_batch = tokens_per_batch // tm
        mod_spec = pl.BlockSpec((None, 1, d), lambda i: (i // tiles_per_batch, 0, 0))
    full = lambda a: pl.BlockSpec(a.shape, lambda i: (0,) * a.ndim)
    row = lambda n: pl.BlockSpec((tm, n), lambda i: (i, 0))
    out_shape = [
        jax.ShapeDtypeStruct((t, sbw), BF16),
        jax.ShapeDtypeStruct((t, sbw), F32),
        jax.ShapeDtypeStruct((t, sbw), F32),
        jax.ShapeDtypeStruct((t, sbw), BF16),
        jax.ShapeDtypeStruct((t, sbw), BF16),
        jax.ShapeDtypeStruct((t, ssdw), F32),
        jax.ShapeDtypeStruct((t, convc), F32),
        jax.ShapeDtypeStruct((t, LANES), F32),
        jax.ShapeDtypeStruct((LANES, t), F32),
    ]
    out_specs = [row(sbw), row(sbw), row(sbw), row(sbw), row(sbw), row(ssdw), row(convc), row(LANES),
                 pl.BlockSpec((LANES, tm), lambda i: (0, i))]
    return pl.pallas_call(
        functools.partial(_inproj_kernel, sbw=sbw, ssdw=ssdw, convc=convc, qscale=qscale),
        grid=grid,
        in_specs=[row(d), mod_spec, mod_spec, full(g), full(w_main), full(w_dt), full(w_dtt), full(dtb),
                  full(dtbt)],
        out_specs=out_specs,
        out_shape=out_shape,
        compiler_params=_cparams(("arbitrary",)),
        name="inproj",
    )(x, shift, scale, g, w_main, w_dt, w_dtt, dtb, dtbt)


def _sb_tile(qm, kb, vb, tri, bias, mask):
    s = _dot_nt(qm, kb) + bias
    e = jnp.exp2(-jnp.abs(s))
    l2 = jnp.log2(1.0 + e)
    ls = jnp.minimum(s, 0.0) - l2
    lr = ls - s
    if mask is not None:
        lr = jnp.where(mask, lr, 0.0)
    later = _dot(lr.astype(BF16), tri)
    w = jnp.exp2(ls + later)
    if mask is not None:
        w = jnp.where(mask, w, 0.0)
    tot = jnp.sum(lr, axis=-1, keepdims=True)
    return _dot(w.astype(BF16), vb), tot


def _strict_upper_ones(n):
    r = lax.broadcasted_iota(I32, (n, n), 0)
    c = lax.broadcasted_iota(I32, (n, n), 1)
    return jnp.where(r > c, 1.0, 0.0).astype(BF16)


def _sb_prompt_kernel(bias_ref, q_ref, k_ref, v_ref, o_ref, *, tq, hd):
    hp = pl.program_id(1)
    qi = pl.program_id(2)
    lane = lax.broadcasted_iota(I32, (tq, LANES), 1)
    r = lax.broadcasted_iota(I32, (tq, tq), 0)
    c = lax.broadcasted_iota(I32, (tq, tq), 1)
    diag_mask = c < r
    tri = _strict_upper_ones(tq)
    q = q_ref[...]
    qms = [jnp.where((lane >= hd * j) & (lane < hd * (j + 1)), q, jnp.zeros_like(q)) for j in range(HEAD_PAIR)]
    biases = [bias_ref[hp * HEAD_PAIR + j] for j in range(HEAD_PAIR)]

    start0 = pl.multiple_of(qi * tq, tq)
    kd = k_ref[pl.ds(start0, tq), :]
    vd = v_ref[pl.ds(start0, tq), :]
    carry = []
    for j in range(HEAD_PAIR):
        pv, tot = _sb_tile(qms[j], kd, vd, tri, biases[j], diag_mask)
        carry += [pv, tot]

    def step(carry, blk):
        start = pl.multiple_of(blk * tq, tq)
        kb = k_ref[pl.ds(start, tq), :]
        vb = v_ref[pl.ds(start, tq), :]
        new = []
        for j in range(HEAD_PAIR):
            acc, cs = carry[2 * j], carry[2 * j + 1]
            pv, tot = _sb_tile(qms[j], kb, vb, tri, biases[j], None)
            new += [acc + jnp.exp2(cs) * pv, cs + tot]
        return tuple(new)

    def body_unrolled(i, carry):
        for u in range(ATTN_UNROLL):
            carry = step(carry, qi - 1 - (i * ATTN_UNROLL + u))
        return carry

    carry = lax.fori_loop(0, qi // ATTN_UNROLL, body_unrolled, tuple(carry))
    rem = qi % ATTN_UNROLL
    carry = lax.fori_loop(0, rem, lambda i, carry: step(carry, rem - 1 - i), carry)
    o_ref[...] = jnp.where(lane < hd, carry[0], carry[2])


def _sb_prompt(q, kb, vb, bias2, *, hd):
    b, t, w = q.shape
    tq = min(ATTN_TQ, t)
    n_pairs = w // LANES
    return pl.pallas_call(
        functools.partial(_sb_prompt_kernel, tq=tq, hd=hd),
        grid=(b, n_pairs, t // tq),
        in_specs=[pl.BlockSpec(memory_space=pltpu.SMEM),
                  pl.BlockSpec((None, tq, LANES), lambda bi, hp, qi: (bi, qi, hp)),
                  pl.BlockSpec((None, t, LANES), lambda bi, hp, qi: (bi, 0, hp)),
                  pl.BlockSpec((None, t, LANES), lambda bi, hp, qi: (bi, 0, hp))],
        out_specs=pl.BlockSpec((None, tq, LANES), lambda bi, hp, qi: (bi, qi, hp)),
        out_shape=jax.ShapeDtypeStruct((b, t, w), F32),
        compiler_params=_cparams(("arbitrary", "arbitrary", "arbitrary")),
        name="sb_prompt",
    )(bias2, q, kb, vb)


def _decode_blocks(qbd, kt_pages, vt_pages, tri, bias, cs, key_mask):
    r = qbd.shape[0]
    n = len(kt_pages)
    s = jnp.concatenate([_dot(qbd, kt) for kt in kt_pages], axis=0) + jnp.concatenate([bias] * n, axis=0)
    e = jnp.exp2(-jnp.abs(s))
    l2 = jnp.log2(1.0 + e)
    ls = jnp.minimum(s, 0.0) - l2
    lr = ls - s
    if key_mask is not None:
        lr = jnp.where(key_mask, lr, 0.0)
    w = jnp.exp2(ls + _dot(lr.astype(BF16), tri))
    if key_mask is not None:
        w = jnp.where(key_mask, w, 0.0)
    tot = jnp.sum(lr, axis=-1, keepdims=True)
    acc = None
    for p, vt in enumerate(vt_pages):
        wp = (w[p * r:(p + 1) * r] * jnp.exp2(cs)).astype(BF16)
        pv = _dot_nt(wp, vt)
        acc = pv if acc is None else acc + pv
        cs = cs + tot[p * r:(p + 1) * r]
    return acc, cs


def _sb_decode_kernel(pt_ref, qbd_ref, bias_ref, kn_ref, vn_ref, *refs, n_pages_step, n_heads, hd, page):
    k_refs = refs[:n_pages_step]
    v_refs = refs[n_pages_step:2 * n_pages_step]
    o_ref = refs[2 * n_pages_step]
    acc_ref, c_ref = refs[2 * n_pages_step + 1:]
    g = pl.program_id(1)
    n_rows = qbd_ref.shape[0]
    qbd = qbd_ref[...]
    bias = bias_ref[...]
    tri = _strict_upper_ones(page)

    @pl.when(g == 0)
    def _():
        rowq = lax.broadcasted_iota(I32, (n_rows, page), 0) // n_heads
        keyi = lax.broadcasted_iota(I32, (n_rows, page), 1)
        pv, tot = _decode_blocks(qbd, [kn_ref[...].astype(BF16)], [vn_ref[...].astype(BF16)], tri, bias,
                                 jnp.zeros((n_rows, 1), F32), keyi < rowq)
        acc_ref[...] = pv
        c_ref[...] = tot

    pv, cs = _decode_blocks(qbd, [k[...].astype(BF16) for k in k_refs], [v[...].astype(BF16) for v in v_refs], tri,
                            bias, c_ref[...], None)
    acc = acc_ref[...] + pv
    acc_ref[...] = acc
    c_ref[...] = cs

    @pl.when(g == pl.num_programs(1) - 1)
    def _():
        w = acc.shape[1]
        hr = lax.broadcasted_iota(I32, (n_rows, w), 0) % n_heads
        hl = lax.broadcasted_iota(I32, (n_rows, w), 1) // hd
        om = jnp.where(hr == hl, acc, 0.0)
        o_ref[...] = jnp.sum(om.reshape(n_rows // n_heads, n_heads, w), axis=1)


def _sb_decode(qbd, bias_col, kt_new, vt_new, cache_kt, cache_vt, page_table, *, n_heads, hd):
    s, n_rows, w = qbd.shape
    page = cache_kt.shape[2]
    n_pages = page_table.shape[1]
    pps = min(DEC_PAGES_PER_STEP, n_pages)
    n_groups = n_pages // pps
    n_tok = n_rows // n_heads

    def page_spec(j):
        return pl.BlockSpec((None, w, page), lambda si, g, pt: (pt[si, n_pages - 1 - (g * pps + j)], 0, 0))

    seq3 = lambda r, c: pl.BlockSpec((None, r, c), lambda si, g, pt: (si, 0, 0))
    grid_spec = pltpu.PrefetchScalarGridSpec(
        num_scalar_prefetch=1,
        grid=(s, n_groups),
        in_specs=[seq3(n_rows, w), pl.BlockSpec((n_rows, 1), lambda si, g, pt: (0, 0)), seq3(w, page), seq3(w, page)]
        + [page_spec(j) for j in range(pps)] + [page_spec(j) for j in range(pps)],
        out_specs=seq3(n_tok, w),
        scratch_shapes=[pltpu.VMEM((n_rows, w), F32), pltpu.VMEM((n_rows, 1), F32)],
    )
    return pl.pallas_call(
        functools.partial(_sb_decode_kernel, n_pages_step=pps, n_heads=n_heads, hd=hd, page=page),
        grid_spec=grid_spec,
        out_shape=jax.ShapeDtypeStruct((s, n_tok, w), F32),
        compiler_params=_cparams(("arbitrary", "arbitrary")),
        name="sb_decode",
    )(page_table, qbd, bias_col, kt_new, vt_new, *([cache_kt] * pps), *([cache_vt] * pps))


def _ssd_kernel(xbc_ref, z_ref, dt_ref, dtt_ref, cprev_ref, s0_ref, cw_ref, cb_ref, arow_ref, acol_ref,
                dskip_ref, g_ref, exp_ref, y_ref, sout_ref, cbuf, state,
                *, q, n_heads, hdim, n_groups, n_state, conv_w):
    c = pl.program_id(1)
    ssdw = n_heads * hdim
    gw = ssdw // n_groups
    heads_per_group = n_heads // n_groups

    @pl.when(c == 0)
    def _():
        cbuf[0:CONV_HALO, :] = cprev_ref[...]
        state[...] = s0_ref[...].T

    cbuf[CONV_HALO:CONV_HALO + q, :] = xbc_ref[...]
    u = cb_ref[...]
    for w in range(conv_w):
        u = u + cw_ref[w:w + 1, :] * cbuf[pl.ds(CONV_HALO - (conv_w - 1) + w, q), :]
    u = _silu(u)
    cbuf[0:CONV_HALO, :] = cbuf[q:q + CONV_HALO, :]
    xs = u[:, 0:ssdw]
    bmat = u[:, ssdw:ssdw + n_groups * n_state]
    cmat = u[:, ssdw + n_groups * n_state:]

    ri = lax.broadcasted_iota(I32, (q, q), 0)
    ci = lax.broadcasted_iota(I32, (q, q), 1)
    low = jnp.where(ci <= ri, 1.0, 0.0).astype(BF16)
    upp = jnp.where(ri <= ci, 1.0, 0.0).astype(BF16)
    dt = dt_ref[...]
    acum = _dot3_right(low, dt * arow_ref[...])
    acum_t = _dot3_left(dtt_ref[...] * acol_ref[...], upp)
    a_last = acum[q - 1:q, :]
    expand = exp_ref[...]
    dt_x = _dot3_left(dt, expand)
    ds_x = _dot3_left(jnp.exp(a_last - acum), expand)
    ea_x = _dot3_left(jnp.exp(acum), expand)
    el_x = _dot3_left(jnp.broadcast_to(jnp.exp(a_last), (SUBLANES, LANES)), expand)[0:1, :]

    xdt = xs * dt_x
    xdt_b = xdt.astype(BF16)
    xst_b = (xdt * ds_x).astype(BF16)
    lane = lax.broadcasted_iota(I32, (q, LANES), 1)
    st = state[...]
    st_b = st.astype(BF16)

    y_diag = []
    y_off = []
    new_state = []
    for g in range(n_groups):
        bg = bmat[:, g * n_state:(g + 1) * n_state].astype(BF16)
        cg = cmat[:, g * n_state:(g + 1) * n_state].astype(BF16)
        cbm = _dot_nt(cg, bg)
        y_off.append(_dot(cg, st_b[:, g * gw:(g + 1) * gw]))
        new_state.append(_dot_tn(bg, xst_b[:, g * gw:(g + 1) * gw]))
        for pr in range(heads_per_group // HEAD_PAIR):
            pair = []
            for j in range(HEAD_PAIR):
                h = g * heads_per_group + pr * HEAD_PAIR + j
                seg = jnp.minimum(acum[:, h:h + 1] - acum_t[h:h + 1, :], 0.0)
                m = jnp.where(ci <= ri, cbm * jnp.exp(seg), 0.0).astype(BF16)
                lo = (h // HEAD_PAIR) * LANES
                pair.append(_dot(m, xdt_b[:, lo:lo + LANES]))
            y_diag.append(jnp.where(lane < hdim, pair[0], pair[1]))

    y = jnp.concatenate(y_diag, axis=-1) + jnp.concatenate(y_off, axis=-1) * ea_x + dskip_ref[...] * xs
    y = y * _silu(z_ref[...])
    normed = []
    for g in range(n_groups):
        yg = y[:, g * gw:(g + 1) * gw]
        normed.append(yg * lax.rsqrt(jnp.mean(yg * yg, axis=-1, keepdims=True) + NORM_EPS))
    y_ref[...] = jnp.concatenate(normed, axis=-1) * g_ref[...]

    st_new = st * el_x + jnp.concatenate(new_state, axis=-1)
    state[...] = st_new

    @pl.when(c == pl.num_programs(1) - 1)
    def _():
        sout_ref[...] = st_new.T


def _ssd(xbc, z, dt, dtt, conv_prev, state0, conv_w, conv_b, a_row, a_col, dskip_row, g_row, expand,
         *, n_heads, hdim, n_groups, n_state):
    b, l, cch = xbc.shape
    q = SSD_CHUNK
    nc = l // q
    ssdw = n_heads * hdim
    kw = conv_w.shape[0]
    full = lambda a: pl.BlockSpec(a.shape, lambda bi, ci: (0,) * a.ndim)
    tok = lambda n: pl.BlockSpec((None, q, n), lambda bi, ci: (bi, ci, 0))
    per_b = lambda r, n: pl.BlockSpec((None, r, n), lambda bi, ci: (bi, 0, 0))
    return pl.pallas_call(
        functools.partial(_ssd_kernel, q=q, n_heads=n_heads, hdim=hdim, n_groups=n_groups, n_state=n_state,
                          conv_w=kw),
        grid=(b, nc),
        in_specs=[tok(cch), tok(ssdw), tok(LANES), pl.BlockSpec((LANES, q), lambda bi, ci: (0, bi * nc + ci)),
                  per_b(CONV_HALO, cch), per_b(ssdw, n_state), full(conv_w), full(conv_b), full(a_row),
                  full(a_col), full(dskip_row), full(g_row), full(expand)],
        out_specs=[tok(ssdw), per_b(ssdw, n_state)],
        out_shape=[jax.ShapeDtypeStruct((b, l, ssdw), F32), jax.ShapeDtypeStruct((b, ssdw, n_state), F32)],
        scratch_shapes=[pltpu.VMEM((q + CONV_HALO, cch), F32), pltpu.VMEM((n_state, ssdw), F32)],
        compiler_params=_cparams(("arbitrary", "arbitrary")),
        name="ssd",
    )(xbc, z, dt, dtt, conv_prev, state0, conv_w, conv_b, a_row, a_col, dskip_row, g_row, expand)


def _post_kernel(x_ref, o_ref, y_ref, gate_ref, sh_ref, sc_ref, gsb_ref, wout_ref, g2_ref, wrh_ref, wrl_ref,
                 br_ref, x1_ref, h2_ref, idx_ref, gt_ref, *, sbw, n_exp):
    o = o_ref[...]
    on = o * lax.rsqrt(jnp.mean(o * o, axis=-1, keepdims=True) + NORM_EPS) * gsb_ref[...]
    out = _dot(on.astype(BF16), wout_ref[0:sbw, :]) + _dot(y_ref[...].astype(BF16), wout_ref[sbw:, :])
    x1 = x_ref[...] + gate_ref[...] * out
    x1_ref[...] = x1
    h2 = x1 * lax.rsqrt(jnp.mean(x1 * x1, axis=-1, keepdims=True) + NORM_EPS) * g2_ref[...]
    h2 = h2 * (1.0 + sc_ref[...]) + sh_ref[...]
    h2_ref[...] = h2

    hh = h2.astype(BF16)
    hl = (h2 - hh.astype(F32)).astype(BF16)
    logits = _dot(hh, wrh_ref[...]) + _dot(hl, wrh_ref[...]) + _dot(hh, wrl_ref[...]) + br_ref[...]
    lane = lax.broadcasted_iota(I32, logits.shape, 1)
    neg = jnp.float32(-jnp.inf)
    lg = jnp.where(lane < n_exp, logits, neg)
    vals, idxs = [], []
    for _ in range(TOP_K):
        m = jnp.max(lg, axis=-1, keepdims=True)
        ix = jnp.min(jnp.where(lg == m, lane, LANES), axis=-1, keepdims=True)
        vals.append(m)
        idxs.append(ix)
        lg = jnp.where(lane == ix, neg, lg)
    es = [jnp.exp(v - vals[0]) for v in vals]
    den = es[0]
    for e in es[1:]:
        den = den + e
    idx_out = jnp.zeros(logits.shape, I32)
    gt_out = jnp.zeros(logits.shape, F32)
    for kk in range(TOP_K):
        idx_out = jnp.where(lane == kk, idxs[kk], idx_out)
        gt_out = jnp.where(lane == kk, es[kk] / den, gt_out)
    idx_ref[...] = idx_out
    gt_ref[...] = gt_out


def _post(x, o_sb, y_ssd, gate, shift, scale, g_sb, w_out, g2, wr_hi, wr_lo, b_router, *, per_row,
          tokens_per_batch, n_exp):
    t, d = x.shape
    sbw = o_sb.shape[1]
    tm = min(PROJ_TM, t)
    if per_row:
        mod_spec = pl.BlockSpec((tm, d), lambda i: (i, 0))
    else:
        tiles_per_batch = tokens_per_batch // tm
        mod_spec = pl.BlockSpec((None, 1, d), lambda i: (i // tiles_per_batch, 0, 0))
    full = lambda a: pl.BlockSpec(a.shape, lambda i: (0,) * a.ndim)
    row = lambda n: pl.BlockSpec((tm, n), lambda i: (i, 0))
    return pl.pallas_call(
        functools.partial(_post_kernel, sbw=sbw, n_exp=n_exp),
        grid=(t // tm,),
        in_specs=[row(d), row(sbw), row(y_ssd.shape[1]), mod_spec, mod_spec, mod_spec, full(g_sb), full(w_out),
                  full(g2), full(wr_hi), full(wr_lo), full(b_router)],
        out_specs=[row(d), row(d), row(LANES), row(LANES)],
        out_shape=[jax.ShapeDtypeStruct((t, d), F32), jax.ShapeDtypeStruct((t, d), F32),
                   jax.ShapeDtypeStruct((t, LANES), I32), jax.ShapeDtypeStruct((t, LANES), F32)],
        compiler_params=_cparams(("arbitrary",)),
        name="post",
    )(x, o_sb, y_ssd, gate, shift, scale, g_sb, w_out, g2, wr_hi, wr_lo, b_router)


def _row_copy(src_hbm, dst, src_row, dst_row, sem):
    return pltpu.make_async_copy(src_hbm.at[pl.ds(src_row, 1)], dst.at[pl.ds(dst_row, 1)], sem)


def _moe_gather_kernel(tok_ref, x_hbm, o_ref, sem, *, bm):
    def issue(r, _):
        _row_copy(x_hbm, o_ref, tok_ref[0, r], r, sem).start()
        return _

    lax.fori_loop(0, bm, issue, 0, unroll=8)

    def drain(r, _):
        _row_copy(x_hbm, o_ref, 0, r, sem).wait()
        return _

    lax.fori_loop(0, bm, drain, 0, unroll=8)


def _moe_gather(x, slot_tok, *, bm):
    nb = slot_tok.shape[0]
    d = x.shape[1]
    return pl.pallas_call(
        functools.partial(_moe_gather_kernel, bm=bm),
        grid=(nb,),
        in_specs=[pl.BlockSpec((None, 1, bm), lambda i: (i, 0, 0), memory_space=pltpu.SMEM),
                  pl.BlockSpec(memory_space=pl.ANY)],
        out_specs=pl.BlockSpec((bm, d), lambda i: (i, 0)),
        out_shape=jax.ShapeDtypeStruct((nb * bm, d), x.dtype),
        scratch_shapes=[pltpu.SemaphoreType.DMA],
        compiler_params=_cparams(("arbitrary",)),
        name="moe_gather",
    )(slot_tok, x)


def _permute_cols_kernel(w_ref, p_ref, o_ref):
    o_ref[...] = _dot(w_ref[...].astype(BF16), p_ref[...]).astype(BF16)


def _permute_cols(w, perm):
    e, r, c = w.shape
    tr = min(PROJ_TM, r)
    return pl.pallas_call(
        _permute_cols_kernel,
        grid=(e, r // tr),
        in_specs=[pl.BlockSpec((None, tr, c), lambda i, j: (i, j, 0)), pl.BlockSpec((c, c), lambda i, j: (0, 0))],
        out_specs=pl.BlockSpec((None, tr, c), lambda i, j: (i, j, 0)),
        out_shape=jax.ShapeDtypeStruct((e, r, c), BF16),
        compiler_params=_cparams(("arbitrary", "arbitrary")),
        name="w1_prep",
    )(w, perm)


def _moe_expert_kernel(be_ref, x_ref, w1_ref, b1_ref, w2_ref, b2_ref, o_ref, *, dff):
    h = _dot(x_ref[...].astype(BF16), w1_ref[...]) + b1_ref[...]
    x_glu = jnp.minimum(h[:, 0:dff], SWIGLU_LIMIT)
    x_lin = jnp.clip(h[:, dff:], -SWIGLU_LIMIT, SWIGLU_LIMIT)
    act = x_glu * jax.nn.sigmoid(SWIGLU_ALPHA * x_glu) * (x_lin + 1.0)
    o_ref[...] = _dot(act.astype(BF16), w2_ref[...]) + b2_ref[...]


def _moe_experts(xs, block_exp, w1p, b1p, w2, b2, *, bm):
    n_slots, d = xs.shape
    dff2 = w1p.shape[2]
    nb = n_slots // bm
    grid_spec = pltpu.PrefetchScalarGridSpec(
        num_scalar_prefetch=1,
        grid=(nb,),
        in_specs=[pl.BlockSpec((bm, d), lambda i, be: (i, 0)),
                  pl.BlockSpec((None, d, dff2), lambda i, be: (be[i], 0, 0)),
                  pl.BlockSpec((None, 1, dff2), lambda i, be: (be[i], 0, 0)),
                  pl.BlockSpec((None, dff2 // 2, d), lambda i, be: (be[i], 0, 0)),
                  pl.BlockSpec((None, 1, d), lambda i, be: (be[i], 0, 0))],
        out_specs=pl.BlockSpec((bm, d), lambda i, be: (i, 0)),
    )
    return pl.pallas_call(
        functools.partial(_moe_expert_kernel, dff=dff2 // 2),
        grid_spec=grid_spec,
        out_shape=jax.ShapeDtypeStruct((n_slots, d), F32),
        compiler_params=_cparams(("arbitrary",)),
        name="moe_experts",
    )(block_exp, xs, w1p, b1p, w2, b2)


def _moe_combine_kernel(slot_ref, eo_hbm, x1_ref, gt_ref, gate_ref, sh_ref, sc_ref, gf_ref, y_ref, buf, sem,
                        *, tm):
    n = tm * TOP_K

    def issue(r, _):
        _row_copy(eo_hbm, buf, slot_ref[0, r], r, sem).start()
        return _

    lax.fori_loop(0, n, issue, 0, unroll=8)

    def drain(r, _):
        _row_copy(eo_hbm, buf, 0, r, sem).wait()
        return _

    lax.fori_loop(0, n, drain, 0, unroll=8)

    gt = gt_ref[...]
    ff = gt[:, 0:1] * buf[0:tm, :]
    for kk in range(1, TOP_K):
        ff = ff + gt[:, kk:kk + 1] * buf[kk * tm:(kk + 1) * tm, :]
    x2 = x1_ref[...] + gate_ref[...] * ff
    h = x2 * lax.rsqrt(jnp.mean(x2 * x2, axis=-1, keepdims=True) + NORM_EPS) * gf_ref[...]
    y_ref[...] = h * (1.0 + sc_ref[...]) + sh_ref[...]


def _moe_combine(eo, slots, x1, gates, gate, shift, scale, g_final, *, per_row, tokens_per_batch):
    t, d = x1.shape
    tm = min(COMB_TM, t)
    if per_row:
        mod_spec = pl.BlockSpec((tm, d), lambda i: (i, 0))
    else:
        tiles_per_batch = tokens_per_batch // tm
        mod_spec = pl.BlockSpec((None, 1, d), lambda i: (i // tiles_per_batch, 0, 0))
    row = lambda n: pl.BlockSpec((tm, n), lambda i: (i, 0))
    return pl.pallas_call(
        functools.partial(_moe_combine_kernel, tm=tm),
        grid=(t // tm,),
        in_specs=[pl.BlockSpec((None, 1, TOP_K * tm), lambda i: (i, 0, 0), memory_space=pltpu.SMEM),
                  pl.BlockSpec(memory_space=pl.ANY), row(d), row(LANES), mod_spec, mod_spec, mod_spec,
                  pl.BlockSpec(g_final.shape, lambda i: (0, 0))],
        out_specs=row(d),
        out_shape=jax.ShapeDtypeStruct((t, d), F32),
        scratch_shapes=[pltpu.VMEM((TOP_K * tm, d), F32), pltpu.SemaphoreType.DMA],
        compiler_params=_cparams(("arbitrary",)),
        name="moe_combine",
    )(slots, eo, x1, gates, gate, shift, scale, g_final)


def _moe_plan(top_idx, n_exp, bm):
    n_tok = top_idx.shape[0]
    n_asg = n_tok * TOP_K
    flat_e = top_idx.reshape(-1)
    onehot = flat_e[:, None] == jnp.arange(n_exp, dtype=I32)[None, :]
    counts = jnp.sum(onehot, axis=0, dtype=I32)
    padded = (counts + bm - 1) // bm * bm
    pad_end = jnp.cumsum(padded)
    pad_start = pad_end - padded
    start = jnp.cumsum(counts) - counts
    order = jnp.argsort(flat_e, stable=True).astype(I32)
    pos = jnp.argsort(order).astype(I32)
    shift = jnp.sum(jnp.where(onehot, (pad_start - start)[None, :], 0), axis=1, dtype=I32)
    slot_of = (pos + shift).reshape(n_tok, TOP_K)
    n_blocks = -(-(n_asg + n_exp * (bm - 1)) // bm)
    blk_start = jnp.arange(n_blocks, dtype=I32) * bm
    block_exp = jnp.minimum(jnp.sum(pad_end[None, :] <= blk_start[:, None], axis=1, dtype=I32), n_exp - 1)
    off = (blk_start - pad_start[block_exp])[:, None] + jnp.arange(bm, dtype=I32)[None, :]
    src = order[jnp.clip(start[block_exp][:, None] + off, 0, n_asg - 1)] // TOP_K
    slot_tok = jnp.where(off < counts[block_exp][:, None], src, 0)
    return slot_tok.reshape(n_blocks, 1, bm), slot_of, block_exp


def _tile_slots(slot_of, tm):
    t, k = slot_of.shape
    return slot_of.reshape(t // tm, tm, k).transpose(0, 2, 1).reshape(t // tm, 1, k * tm)


def kernel(x_prompt, x_sample, c_prompt, c_sample, cache_k, cache_v, state_ssm, state_conv, page_table, w_ada, b_ada, g_norm1, w_in, conv_w, conv_b, dt_bias, a_log, d_skip, sb_bias, g_sb, g_ssd, w_out, g_norm2, w_router, b_router, w1, b1, w2, b2, w_ada_final, b_ada_final, g_final):
    bp, tp, d = x_prompt.shape
    bs, ts, _ = x_sample.shape
    depth = w_in.shape[0]
    assert depth == 1, "the MoE combine kernel fuses the final norm, so only one layer is supported"
    n_sb_heads, hd = cache_k.shape[3], cache_k.shape[4]
    sbw = n_sb_heads * hd
    page = cache_k.shape[2]
    n_heads, hdim, n_state = state_ssm.shape[2], state_ssm.shape[3], state_ssm.shape[4]
    ssdw = n_heads * hdim
    convc = state_conv.shape[3]
    kw = conv_w.shape[1]
    n_groups = (convc - ssdw) // (2 * n_state)
    n_exp = w_router.shape[2]
    dff = w2.shape[2]
    n_p, n_s = bp * tp, bs * ts

    c_all = jnp.concatenate([c_prompt, c_sample], axis=0)
    pad_r = (-c_all.shape[0]) % SUBLANES
    c_all = jnp.pad(c_all, ((0, pad_r), (0, 0)))
    m_fin = _ada(c_all, w_ada_final, b_ada_final)

    xp = x_prompt.reshape(n_p, d)
    xs = x_sample.reshape(n_s, d)
    qscale = LOG2E / math.sqrt(hd)
    expand = (jnp.arange(LANES)[:, None] == (jnp.arange(ssdw)[None, :] // hdim)).astype(BF16)
    n_pool = cache_k.shape[1]
    cache_k2 = cache_k.transpose(0, 1, 3, 4, 2).reshape(depth * n_pool, sbw, page)
    cache_v2 = cache_v.transpose(0, 1, 3, 4, 2).reshape(depth * n_pool, sbw, page)

    def prompt_mod(mm, i):
        return mm[:bp, i * d:(i + 1) * d].reshape(bp, 1, d)

    def sample_mod(mm, i):
        return jnp.repeat(mm[bp:bp + bs, i * d:(i + 1) * d], ts, axis=0)

    outs = {n: [] for n in ("kp", "vp", "sp", "cp", "ks", "vs", "ss", "cs")}
    for l in range(depth):
        m_all = _ada(c_all, w_ada[l], b_ada[l])
        w_main = w_in[l][:, :3 * sbw + ssdw + convc].astype(BF16)
        w_dt = jnp.pad(w_in[l][:, 3 * sbw + ssdw + convc:], ((0, 0), (0, LANES - n_heads))).astype(BF16)
        w_dtt = w_dt.T
        dtb = jnp.pad(dt_bias[l], (0, LANES - n_heads)).reshape(1, LANES)
        a_neg = jnp.pad(-jnp.exp(a_log[l]), (0, LANES - n_heads))
        g1 = g_norm1[l].reshape(1, d)
        proj = functools.partial(_inproj, g=g1, w_main=w_main, w_dt=w_dt, w_dtt=w_dtt, dtb=dtb,
                                 dtbt=dtb.reshape(LANES, 1), sbw=sbw, ssdw=ssdw, convc=convc, qscale=qscale)
        ssd = functools.partial(
            _ssd, conv_w=conv_w[l], conv_b=conv_b[l].reshape(1, convc), a_row=a_neg.reshape(1, LANES),
            a_col=a_neg.reshape(LANES, 1), dskip_row=jnp.repeat(d_skip[l], hdim).reshape(1, ssdw),
            g_row=g_ssd[l].reshape(1, ssdw), expand=expand, n_heads=n_heads, hdim=hdim, n_groups=n_groups,
            n_state=n_state)
        wr = jnp.pad(w_router[l], ((0, 0), (0, LANES - n_exp)))
        wr_hi = wr.astype(BF16)
        wr_lo = (wr - wr_hi.astype(F32)).astype(BF16)
        post = functools.partial(
            _post, g_sb=g_sb[l].reshape(1, sbw), w_out=w_out[l].astype(BF16), g2=g_norm2[l].reshape(1, d),
            wr_hi=wr_hi, wr_lo=wr_lo, b_router=jnp.pad(b_router[l], (0, LANES - n_exp)).reshape(1, LANES),
            n_exp=n_exp)
        bias2 = sb_bias[l] * LOG2E

        q_p, k_p, v_p, kb_p, vb_p, z_p, xbc_p, dt_p, dtt_p = proj(
            xp, prompt_mod(m_all, 0), prompt_mod(m_all, 1), per_row=False, tokens_per_batch=tp)
        o_p = _sb_prompt(q_p.reshape(bp, tp, sbw), kb_p.reshape(bp, tp, sbw), vb_p.reshape(bp, tp, sbw), bias2,
                         hd=hd)
        y_p, s_p = ssd(xbc_p.reshape(bp, tp, convc), z_p.reshape(bp, tp, ssdw), dt_p.reshape(bp, tp, LANES),
                       dtt_p, jnp.zeros((bp, CONV_HALO, convc), F32), jnp.zeros((bp, ssdw, n_state), F32))
        x1_p, h2_p, idx_p, gt_p = post(xp, o_p.reshape(n_p, sbw), y_p.reshape(n_p, ssdw), prompt_mod(m_all, 2),
                                       prompt_mod(m_all, 3), prompt_mod(m_all, 4), per_row=False,
                                       tokens_per_batch=tp)

        q_s, k_s, v_s, _, _, z_s, xbc_s, dt_s, dtt_s = proj(
            xs, sample_mod(m_all, 0), sample_mod(m_all, 1), per_row=True, tokens_per_batch=ts)
        head_of_lane = jnp.arange(sbw) // hd
        qbd = jnp.where(head_of_lane[None, None, None, :] == jnp.arange(n_sb_heads)[None, None, :, None],
                        q_s.reshape(bs, ts, 1, sbw), jnp.zeros((), BF16)).reshape(bs, ts * n_sb_heads, sbw)
        bias_col = jnp.tile(bias2, ts).reshape(ts * n_sb_heads, 1)
        pad_keys = lambda a: jnp.pad(a.reshape(bs, ts, sbw).transpose(0, 2, 1), ((0, 0), (0, 0), (0, page - ts)))
        o_s = _sb_decode(qbd, bias_col, pad_keys(k_s), pad_keys(v_s), cache_k2, cache_v2,
                         page_table + l * n_pool, n_heads=n_sb_heads, hd=hd)
        pad_tok = lambda a, n: jnp.pad(a.reshape(bs, ts, n), ((0, 0), (0, SSD_CHUNK - ts), (0, 0)))
        dtt_s_pad = jnp.pad(dtt_s.reshape(LANES, bs, ts), ((0, 0), (0, 0), (0, SSD_CHUNK - ts))).reshape(
            LANES, bs * SSD_CHUNK)
        cprev_s = jnp.pad(state_conv[l], ((0, 0), (CONV_HALO - (kw - 1), 0), (0, 0)))
        y_s, s_s = ssd(pad_tok(xbc_s, convc), pad_tok(z_s, ssdw), pad_tok(dt_s, LANES), dtt_s_pad, cprev_s,
                       state_ssm[l].reshape(bs, ssdw, n_state))
        y_s = y_s[:, :ts].reshape(n_s, ssdw)
        x1_s, h2_s, idx_s, gt_s = post(xs, o_s.reshape(n_s, sbw), y_s, sample_mod(m_all, 2), sample_mod(m_all, 3),
                                       sample_mod(m_all, 4), per_row=True, tokens_per_batch=ts)

        outs["kp"].append(k_p.reshape(bp, tp, n_sb_heads, hd))
        outs["vp"].append(v_p.reshape(bp, tp, n_sb_heads, hd))
        outs["sp"].append(s_p.reshape(bp, n_heads, hdim, n_state).astype(state_ssm.dtype))
        outs["cp"].append(xbc_p.reshape(bp, tp, convc)[:, tp - (kw - 1):])
        outs["ks"].append(k_s.reshape(bs, ts, n_sb_heads, hd))
        outs["vs"].append(v_s.reshape(bs, ts, n_sb_heads, hd))
        outs["ss"].append(s_s.reshape(bs, n_heads, hdim, n_state).astype(state_ssm.dtype))
        full_s = jnp.concatenate([state_conv[l], xbc_s.reshape(bs, ts, convc)], axis=1)
        outs["cs"].append(full_s[:, full_s.shape[1] - (kw - 1):])

        h2 = jnp.concatenate([h2_p, h2_s], axis=0)
        top_idx = jnp.concatenate([idx_p[:, :TOP_K], idx_s[:, :TOP_K]], axis=0)
        slot_tok, slot_of, block_exp = _moe_plan(top_idx, n_exp, MOE_BM)
        col = jnp.arange(2 * dff, dtype=I32)
        perm = (col[None, :] == (col // 2 + (col % 2) * dff)[:, None]).astype(BF16)
        w1p = _permute_cols(w1[l], perm)
        b1p = jnp.concatenate([b1[l][:, 0::2], b1[l][:, 1::2]], axis=-1).reshape(n_exp, 1, 2 * dff)
        x_slots = _moe_gather(h2, slot_tok, bm=MOE_BM)
        eo = _moe_experts(x_slots, block_exp, w1p, b1p, w2[l].astype(BF16), b2[l].reshape(n_exp, 1, d), bm=MOE_BM)
        gf = g_final.reshape(1, d)
        tmp = min(COMB_TM, n_p)
        tms = min(COMB_TM, n_s)
        xp = _moe_combine(eo, _tile_slots(slot_of[:n_p], tmp), x1_p, gt_p, prompt_mod(m_all, 5),
                          prompt_mod(m_fin, 0), prompt_mod(m_fin, 1), gf, per_row=False, tokens_per_batch=tp)
        xs = _moe_combine(eo, _tile_slots(slot_of[n_p:], tms), x1_s, gt_s, sample_mod(m_all, 5),
                          sample_mod(m_fin, 0), sample_mod(m_fin, 1), gf, per_row=True, tokens_per_batch=ts)

    st = lambda n: jnp.stack(outs[n])
    return (xp.reshape(bp, tp, d), xs.reshape(bs, ts, d), st("kp"), st("vp"), st("sp"), st("cp"),
            st("ks"), st("vs"), st("ss"), st("cs"))
```
